```python
import math, functools
import jax, jax.numpy as jnp
from jax import lax
import numpy as np

D_MODEL = 1024
BATCH = 16
SEQ = 2048
DEPTH = 2
DEC_BATCH = 32
DEC_SEQ = 4
PAST_LEN = 16384
PAGE_SIZE = 128

H_A = 4
DH = 64
VD = 2 * DH
H_B = 4
DK = 128
DV = 128
HGRN_CHUNK = 64
NUM_BUCKETS = 32
MAX_DISTANCE = 128
N_MEM = 256
MH = 4
MDH = 128
PEER_HEADS = 8
N_KEYS = 128
N_EXPERTS = N_KEYS * N_KEYS
PEER_QDIM = 256
PEER_TOPK = 16
PEER_BLOCK = 256
Q_BLOCK = 128
EPS = 1e-6
NEG = -1e30
F_MIN = 1e-20
IN_SIZES = (H_A * 2 * DH, H_A * 2 * DH, H_A * VD, H_B * DK, H_B * DK, H_B * DV, H_B * DV)
IN_COLS = H_A * 2 * DH * 2 + H_A * VD + H_B * DK * 2 + H_B * DV * 2

kernel_name = "hymba_diffattn_hgrn2_peer_step"


def rmsnorm(x, g):
    xf = x.astype(jnp.float32)
    y = xf * lax.rsqrt(jnp.mean(xf * xf, axis=-1, keepdims=True) + EPS)
    return (y * g.astype(jnp.float32)).astype(x.dtype)


def t5_bucket(rel):
    n = jnp.maximum(rel, 0)
    max_exact = NUM_BUCKETS // 2
    nf = jnp.maximum(n, max_exact).astype(jnp.float32)
    large = max_exact + (jnp.log(nf / max_exact) / math.log(MAX_DISTANCE / max_exact)
                         * (NUM_BUCKETS - max_exact)).astype(jnp.int32)
    large = jnp.minimum(large, NUM_BUCKETS - 1)
    return jnp.where(n < max_exact, n, large)


def diff_attn_core(q, k, v, q_pos, k_pos, lam, rel_bias):
    s = jnp.einsum('bqhmd,bkhmd->bhmqk', q.astype(jnp.float32), k.astype(jnp.float32)) * (DH ** -0.5)
    rel = q_pos[:, None] - k_pos[None, :]
    bias = jnp.transpose(rel_bias[t5_bucket(rel)].astype(jnp.float32), (2, 0, 1))
    s = jnp.where(rel >= 0, s + bias[None, :, None], NEG)
    p = jax.nn.softmax(s, axis=-1).astype(v.dtype)
    o = jnp.einsum('bhmqk,bkhv->bqhmv', p, v)
    return o[..., 0, :] - lam.astype(o.dtype) * o[..., 1, :]


def prompt_attn(q, k, v, lam, rel_bias):
    B, T = q.shape[:2]
    qb_len = min(Q_BLOCK, T)
    nb = T // qb_len
    qb = q.reshape(B, nb, qb_len, H_A, 2, DH).swapaxes(0, 1)
    starts = jnp.arange(nb, dtype=jnp.int32) * qb_len
    k_pos = jnp.arange(T, dtype=jnp.int32)

    def block(args):
        qi, st = args
        return diff_attn_core(qi, k, v, st + jnp.arange(qb_len, dtype=jnp.int32), k_pos, lam, rel_bias)

    o = lax.map(block, (qb, starts))
    return o.swapaxes(0, 1).reshape(B, T, H_A, VD)


def sample_attn(q, k, v, lam, rel_bias, k_past, v_past):
    past = k_past.shape[1]
    T = q.shape[1]
    kk = jnp.concatenate([k_past.astype(k.dtype), k], axis=1)
    vv = jnp.concatenate([v_past.astype(v.dtype), v], axis=1)
    q_pos = past + jnp.arange(T, dtype=jnp.int32)
    k_pos = jnp.arange(past + T, dtype=jnp.int32)
    return diff_attn_core(q, kk, vv, q_pos, k_pos, lam, rel_bias)


def hgrn2_chunked(q, k, v, logf, s0, chunk):
    B, T, H, _ = q.shape
    nc = T // chunk

    def to_chunks(a):
        return a.reshape(B, nc, chunk, H, a.shape[-1]).swapaxes(0, 1)

    causal = jnp.tril(jnp.ones((chunk, chunk), dtype=bool))

    def step(S, inp):
        qc, kc, vc, gc = inp
        b = jnp.cumsum(gc, axis=1)
        inter = jnp.einsum('bthk,bhkv->bthv', qc * jnp.exp(b), S)
        diff = b[:, :, None] - b[:, None, :]
        decay = jnp.exp(jnp.where(causal[None, :, :, None, None], diff, NEG))
        a = jnp.einsum('bthk,bshk,btshk->bhts', qc, kc, decay)
        intra = jnp.einsum('bhts,bshv->bthv', a, vc)
        b_end = b[:, -1]
        S_new = jnp.exp(b_end)[..., None] * S + jnp.einsum('bshk,bshv->bhkv', kc * jnp.exp(b_end[:, None] - b), vc)
        return S_new, inter + intra

    S_fin, o = lax.scan(step, s0, (to_chunks(q), to_chunks(k), to_chunks(v), to_chunks(logf)))
    return o.swapaxes(0, 1).reshape(B, T, H, v.shape[-1]), S_fin


def peer_ffn(h, wq, subkeys, u, v):
    D = h.shape[-1]
    hf = h.reshape(-1, D)
    n = hf.shape[0]
    qp = (hf @ wq).reshape(n, PEER_HEADS, 2, PEER_QDIM // 2)
    s = jnp.einsum('nhpd,hpkd->nhpk', qp.astype(jnp.float32), subkeys.astype(jnp.float32))
    s1, i1 = lax.top_k(s[:, :, 0], PEER_TOPK)
    s2, i2 = lax.top_k(s[:, :, 1], PEER_TOPK)
    cand = (s1[..., :, None] + s2[..., None, :]).reshape(n, PEER_HEADS, PEER_TOPK * PEER_TOPK)
    cs, ci = lax.top_k(cand, PEER_TOPK)
    idx = (jnp.take_along_axis(i1, ci // PEER_TOPK, axis=-1) * N_KEYS
           + jnp.take_along_axis(i2, ci % PEER_TOPK, axis=-1))
    gate = jax.nn.softmax(cs, axis=-1).astype(h.dtype)
    n_pad = -(-n // PEER_BLOCK) * PEER_BLOCK
    pad = n_pad - n
    nb = n_pad // PEER_BLOCK
    hp = jnp.pad(hf, ((0, pad), (0, 0))).reshape(nb, PEER_BLOCK, D)
    ip = jnp.pad(idx, ((0, pad), (0, 0), (0, 0))).reshape(nb, PEER_BLOCK, PEER_HEADS, PEER_TOPK)
    gp = jnp.pad(gate, ((0, pad), (0, 0), (0, 0))).reshape(nb, PEER_BLOCK, PEER_HEADS, PEER_TOPK)

    def apply(args):
        hb, ib, gb = args
        act = jax.nn.gelu(jnp.einsum('nd,nhkd->nhk', hb, u[ib]), approximate=False)
        return jnp.einsum('nhk,nhkd->nd', gb * act, v[ib])

    out = lax.map(apply, (hp, ip, gp)).reshape(n_pad, D)[:n]
    return out.reshape(h.shape).astype(h.dtype)


def memory_kv(mem, mem_norm, w_mk, w_mv, mk_gain):
    Bm, M, _ = mem.shape
    mn = rmsnorm(mem, mem_norm)
    k = rmsnorm((mn @ w_mk).reshape(Bm, M, MH, MDH), mk_gain)
    v = (mn @ w_mv).reshape(Bm, M, MH, MDH)
    return k, v


def trunk_layer(x, attn_fn, s0, mem_k, mem_v, chunk, lambda_init, lb,
                ln_mix, w_in, qk_gain, lambda_qk, subln_gain, hgrn_norm, w_out,
                ln_mem, w_mq, mq_gain, w_mo, ln_ffn, peer_wq, peer_subkeys, peer_u, peer_v):
    B, T, _ = x.shape
    f32 = jnp.float32
    h = rmsnorm(x, ln_mix)
    cuts = [sum(IN_SIZES[:i + 1]) for i in range(len(IN_SIZES) - 1)]
    qa, ka, va, qh, fh, ih, gh = jnp.split(h @ w_in, cuts, axis=-1)
    qa = rmsnorm(qa.reshape(B, T, H_A, 2, DH), qk_gain[0])
    ka = rmsnorm(ka.reshape(B, T, H_A, 2, DH), qk_gain[1])
    va = va.reshape(B, T, H_A, VD)
    lq1, lk1, lq2, lk2 = lambda_qk.astype(f32)
    lam = jnp.exp(jnp.sum(lq1 * lk1)) - jnp.exp(jnp.sum(lq2 * lk2)) + lambda_init
    o_a = rmsnorm(attn_fn(qa, ka, va, lam), subln_gain) * (1.0 - lambda_init)
    o_a = o_a.reshape(B, T, H_A * VD)
    fx = fh.reshape(B, T, H_B, DK).astype(f32)
    f = lb + (1.0 - lb) * jax.nn.sigmoid(fx)
    logf = jnp.log(jnp.maximum(f, F_MIN))
    kh = 1.0 - f
    o_b, s_new = hgrn2_chunked(qh.reshape(B, T, H_B, DK).astype(f32), kh,
                               ih.reshape(B, T, H_B, DV).astype(f32), logf, s0.astype(f32), chunk)
    o_b = rmsnorm(o_b.astype(x.dtype), hgrn_norm).reshape(B, T, H_B * DV) * jax.nn.silu(gh)
    x = x + jnp.concatenate([o_a, o_b], axis=-1) @ w_out
    h = rmsnorm(x, ln_mem)
    qm = rmsnorm((h @ w_mq).reshape(B, T, MH, MDH), mq_gain)
    sm = jnp.einsum('bqhd,bkhd->bhqk', qm.astype(f32), mem_k.astype(f32)) * (MDH ** -0.5)
    pm = jax.nn.softmax(sm, axis=-1).astype(mem_v.dtype)
    om = jnp.einsum('bhqk,bkhd->bqhd', pm, mem_v).reshape(B, T, MH * MDH).astype(x.dtype)
    x = x + om @ w_mo
    h = rmsnorm(x, ln_ffn)
    x = x + peer_ffn(h, peer_wq, peer_subkeys, peer_u, peer_v)
    return x, ka, va, s_new


def setup_inputs(seed: int = 0) -> dict:
    key = jax.random.key(seed)
    ks = jax.random.split(key, 32)
    f32 = jnp.float32
    D = D_MODEL
    n_pages = PAST_LEN // PAGE_SIZE
    n_phys = (DEC_BATCH * n_pages * 5) // 4
    nrm = lambda k, shape, s=1.0: (jax.random.normal(k, shape, f32) * s)
    gain = lambda k, shape: 1.0 + 0.1 * jax.random.normal(k, shape, f32)
    page_table = jax.random.permutation(ks[0], n_phys)[:DEC_BATCH * n_pages].reshape(DEC_BATCH, n_pages).astype(jnp.int32)
    return {
        "x_prompt": nrm(ks[1], (BATCH, SEQ, D)),
        "x_sample": nrm(ks[2], (DEC_BATCH, DEC_SEQ, D)),
        "cache_k": nrm(ks[3], (DEPTH, n_phys, PAGE_SIZE, H_A, 2, DH)),
        "cache_v": nrm(ks[4], (DEPTH, n_phys, PAGE_SIZE, H_A, VD)),
        "cache_mem_k": nrm(ks[5], (DEPTH, DEC_BATCH, N_MEM, MH, MDH)),
        "cache_mem_v": nrm(ks[6], (DEPTH, DEC_BATCH, N_MEM, MH, MDH)),
        "state_hgrn": nrm(ks[7], (DEPTH, DEC_BATCH, H_B, DK, DV), 0.5),
        "page_table": page_table,
        "mem_prompt": nrm(ks[8], (BATCH, N_MEM, D)),
        "ln_mix": gain(ks[9], (DEPTH, D)),
        "w_in": nrm(ks[10], (DEPTH, D, IN_COLS), D ** -0.5),
        "qk_gain": gain(ks[11], (DEPTH, 2, DH)),
        "lambda_qk": nrm(ks[12], (DEPTH, 4, DH), 0.1),
        "subln_gain": gain(ks[13], (DEPTH, VD)),
        "rel_bias": nrm(ks[14], (NUM_BUCKETS, H_A), 0.5),
        "lower_bounds": nrm(ks[15], (DEPTH, H_B * DK)),
        "hgrn_norm": gain(ks[16], (DEPTH, DV)),
        "w_out": nrm(ks[17], (DEPTH, H_A * VD + H_B * DV, D), (H_A * VD + H_B * DV) ** -0.5),
        "ln_mem": gain(ks[18], (DEPTH, D)),
        "mem_norm": gain(ks[19], (DEPTH, D)),
        "w_mq": nrm(ks[20], (DEPTH, D, MH * MDH), D ** -0.5),
        "w_mk": nrm(ks[21], (DEPTH, D, MH * MDH), D ** -0.5),
        "w_mv": nrm(ks[22], (DEPTH, D, MH * MDH), D ** -0.5),
        "mem_qk_gain": gain(ks[23], (DEPTH, 2, MDH)),
        "w_mo": nrm(ks[24], (DEPTH, MH * MDH, D), (MH * MDH) ** -0.5),
        "ln_ffn": gain(ks[25], (DEPTH, D)),
        "peer_wq": nrm(ks[26], (DEPTH, D, PEER_HEADS * PEER_QDIM), D ** -0.5),
        "peer_subkeys": nrm(ks[27], (DEPTH, PEER_HEADS, 2, N_KEYS, PEER_QDIM // 2), (PEER_QDIM // 2) ** -0.5),
        "peer_u": nrm(ks[28], (DEPTH, N_EXPERTS, D), D ** -0.5),
        "peer_v": nrm(ks[29], (DEPTH, N_EXPERTS, D), 0.3),
    }


def reference(x_prompt, x_sample, cache_k, cache_v, cache_mem_k, cache_mem_v, state_hgrn, page_table,
              mem_prompt, ln_mix, w_in, qk_gain, lambda_qk, subln_gain, rel_bias, lower_bounds, hgrn_norm,
              w_out, ln_mem, mem_norm, w_mq, w_mk, w_mv, mem_qk_gain, w_mo, ln_ffn, peer_wq, peer_subkeys,
              peer_u, peer_v):
    B, T, _ = x_prompt.shape
    DB, TS, _ = x_sample.shape
    past = page_table.shape[1] * PAGE_SIZE
    lbp = jax.nn.softmax(lower_bounds.astype(jnp.float32), axis=0)
    lbs = jnp.cumsum(lbp, axis=0) - lbp[0]
    p_attn = functools.partial(prompt_attn, rel_bias=rel_bias)
    yp, ys = x_prompt, x_sample
    kp_l, vp_l, mkp_l, mvp_l, sp_l, ks_l, vs_l, ss_l = [], [], [], [], [], [], [], []
    for l in range(DEPTH):
        lambda_init = 0.8 - 0.6 * math.exp(-0.3 * l)
        lb = lbs[l].reshape(H_B, DK)
        layer_w = (ln_mix[l], w_in[l], qk_gain[l], lambda_qk[l], subln_gain[l], hgrn_norm[l], w_out[l],
                   ln_mem[l], w_mq[l], mem_qk_gain[l, 0], w_mo[l], ln_ffn[l], peer_wq[l], peer_subkeys[l],
                   peer_u[l], peer_v[l])
        mk_p, mv_p = memory_kv(mem_prompt, mem_norm[l], w_mk[l], w_mv[l], mem_qk_gain[l, 1])
        s0_p = jnp.zeros((B, H_B, DK, DV), jnp.float32)
        yp, kp, vp, sp = trunk_layer(yp, p_attn, s0_p, mk_p, mv_p, min(HGRN_CHUNK, T), lambda_init, lb, *layer_w)
        k_past = cache_k[l, page_table].reshape(DB, past, H_A, 2, DH)
        v_past = cache_v[l, page_table].reshape(DB, past, H_A, VD)
        s_attn = functools.partial(sample_attn, rel_bias=rel_bias, k_past=k_past, v_past=v_past)
        ys, kss, vss, ss = trunk_layer(ys, s_attn, state_hgrn[l], cache_mem_k[l], cache_mem_v[l], TS,
                                       lambda_init, lb, *layer_w)
        kp_l.append(kp); vp_l.append(vp); mkp_l.append(mk_p); mvp_l.append(mv_p)
        sp_l.append(sp.astype(state_hgrn.dtype))
        ks_l.append(kss); vs_l.append(vss); ss_l.append(ss.astype(state_hgrn.dtype))
    return (yp, ys, jnp.stack(kp_l), jnp.stack(vp_l), jnp.stack(mkp_l), jnp.stack(mvp_l), jnp.stack(sp_l),
            jnp.stack(ks_l), jnp.stack(vs_l), jnp.stack(ss_l))
```

```python
import functools
import math

import numpy as np
import jax
import jax.numpy as jnp
from jax import lax
from jax.experimental import pallas as pl
from jax.experimental.pallas import tpu as pltpu

F32 = jnp.float32
BF16 = jnp.bfloat16

PAGE_SIZE = 128
H_A = 4
DH = 64
VD = 2 * DH
H_B = 4
DK = 128
DV = 128
NUM_BUCKETS = 32
MAX_DISTANCE = 128
MH = 4
MDH = 128
PEER_HEADS = 8
N_KEYS = 128
PEER_TOPK = 16
EPS = 1e-6
NEG = -1e30
F_MIN = 1e-20
HA_COLS = H_A * 2 * DH
SQRT_HALF = 0.7071067811865476

VMEM_LIMIT_BYTES = 56 * 1024 * 1024
HGRN_CHUNK = 64
HGRN_LEVELS = (1, 2, 4, 8, 16, 32, 64)


def _cparams(sem):
    return pltpu.CompilerParams(dimension_semantics=sem, vmem_limit_bytes=VMEM_LIMIT_BYTES)


def _rms(x, g):
    return x * lax.rsqrt(jnp.mean(x * x, axis=-1, keepdims=True) + EPS) * g


def _group_mean_sq(x, p_ref):
    sq = x * x
    hi = sq.astype(BF16)
    lo = (sq - hi.astype(F32)).astype(BF16)
    p = p_ref[...]
    return jnp.dot(hi, p, preferred_element_type=F32) + jnp.dot(lo, p, preferred_element_type=F32)


def _dot_nt(a, b):
    return lax.dot_general(a, b, (((1,), (1,)), ((), ())), preferred_element_type=F32)


def _t5_bucket(rel):
    n = jnp.maximum(rel, 0)
    max_exact = NUM_BUCKETS // 2
    nf = jnp.maximum(n, max_exact).astype(F32)
    large = max_exact + (jnp.log(nf / max_exact) / math.log(MAX_DISTANCE / max_exact)
                         * (NUM_BUCKETS - max_exact)).astype(jnp.int32)
    large = jnp.minimum(large, NUM_BUCKETS - 1)
    return jnp.where(n < max_exact, n, large)


def _in_proj_kernel(x_ref, ln_ref, w_ref, gq_ref, gk_ref, p_ref,
                    qab_ref, ka_ref, va_ref, kab_ref, vab_ref, qh_ref, fh_ref, ih_ref, gh_ref):
    hb = _rms(x_ref[...], ln_ref[...]).astype(BF16)

    def proj(j):
        return jnp.dot(hb, w_ref[:, j * HA_COLS:(j + 1) * HA_COLS], preferred_element_type=F32)

    qa = proj(0)
    qn = qa * lax.rsqrt(_group_mean_sq(qa, p_ref) + EPS) * gq_ref[...]
    qab_ref[...] = (qn * (DH ** -0.5)).astype(BF16)
    ka = proj(1)
    kn = ka * lax.rsqrt(_group_mean_sq(ka, p_ref) + EPS) * gk_ref[...]
    ka_ref[...] = kn
    kab_ref[...] = kn.astype(BF16)
    va = proj(2)
    va_ref[...] = va
    vab_ref[...] = va.astype(BF16)
    qh_ref[...] = proj(3)
    fh_ref[...] = proj(4)
    ih_ref[...] = proj(5)
    gh_ref[...] = proj(6)


def _in_proj(x2d, ln, w_b, gq, gk, p64, tm):
    n, d = x2d.shape
    cols = w_b.shape[1]
    row = lambda i: (i, 0)
    const = lambda i: (0, 0)
    o_spec = pl.BlockSpec((tm, HA_COLS), row)
    f32o = jax.ShapeDtypeStruct((n, HA_COLS), F32)
    b16o = jax.ShapeDtypeStruct((n, HA_COLS), BF16)
    return pl.pallas_call(
        _in_proj_kernel,
        grid=(n // tm,),
        in_specs=[pl.BlockSpec((tm, d), row), pl.BlockSpec((1, d), const), pl.BlockSpec((d, cols), const),
                  pl.BlockSpec((1, HA_COLS), const), pl.BlockSpec((1, HA_COLS), const),
                  pl.BlockSpec((HA_COLS, HA_COLS), const)],
        out_specs=[o_spec] * 9,
        out_shape=[b16o, f32o, f32o, b16o, b16o, f32o, f32o, f32o, f32o],
        compiler_params=_cparams(("parallel",)),
        name="in_proj",
    )(x2d, ln, w_b, gq, gk, p64)


def _bias_tile_kernel(rb_ref, o_ref, *, tb):
    h = pl.program_id(0)
    d = pl.program_id(1)
    r = lax.broadcasted_iota(jnp.int32, (tb, tb), 0)
    c = lax.broadcasted_iota(jnp.int32, (tb, tb), 1)
    bucket = _t5_bucket(d * tb + r - c)
    acc = jnp.full((tb, tb), rb_ref[h], F32)
    for b in range(1, NUM_BUCKETS):
        acc = jnp.where(bucket == b, rb_ref[b * H_A + h], acc)
    o_ref[0, 0] = acc


def _bias_tiles(rel_bias, tb):
    return pl.pallas_call(
        functools.partial(_bias_tile_kernel, tb=tb),
        grid=(H_A, 2),
        in_specs=[pl.BlockSpec(memory_space=pltpu.SMEM)],
        out_specs=pl.BlockSpec((1, 1, tb, tb), lambda h, d: (h, d, 0, 0)),
        out_shape=jax.ShapeDtypeStruct((H_A, 2, tb, tb), F32),
        compiler_params=_cparams(("parallel", "parallel")),
        name="bias_tiles",
    )(rel_bias.reshape(-1))


def _prompt_attn_kernel(lam_ref, rb_ref, q_ref, k_ref, v_ref, bias_ref, g_ref, o_ref, *, tq, out_scale):
    h = pl.program_id(1)
    i = pl.program_id(2)
    q = q_ref[0]
    lane = lax.broadcasted_iota(jnp.int32, (tq, 2 * DH), 1)
    zero = jnp.zeros_like(q)
    q2 = jnp.concatenate([jnp.where(lane < DH, q, zero), jnp.where(lane >= DH, q, zero)], axis=0)
    far_bias = rb_ref[(NUM_BUCKETS - 1) * H_A + h]

    def update(carry, j, s_fn):
        m, l, acc = carry
        kj = k_ref[0, pl.ds(pl.multiple_of(j * tq, tq), tq), :]
        vj = v_ref[0, pl.ds(pl.multiple_of(j * tq, tq), tq), :]
        s = s_fn(_dot_nt(q2, kj))
        m_new = jnp.maximum(m, jnp.max(s, axis=-1, keepdims=True))
        alpha = jnp.exp(m - m_new)
        p = jnp.exp(s - m_new)
        l = alpha * l + jnp.sum(p, axis=-1, keepdims=True)
        acc = alpha * acc + jnp.dot(p.astype(BF16), vj, preferred_element_type=F32)
        return m_new, l, acc

    init = (jnp.full((2 * tq, 1), NEG, F32), jnp.zeros((2 * tq, 1), F32), jnp.zeros((2 * tq, VD), F32))
    carry = lax.fori_loop(0, jnp.maximum(i - 1, 0),
                          lambda j, c: update(c, j, lambda s: s + far_bias), init)
    b1 = bias_ref[0, 1]
    b1 = jnp.concatenate([b1, b1], axis=0)
    carry = update(carry, jnp.maximum(i - 1, 0), lambda s: jnp.where(i >= 1, s + b1, NEG))
    b0 = bias_ref[0, 0]
    r = lax.broadcasted_iota(jnp.int32, (tq, tq), 0)
    c = lax.broadcasted_iota(jnp.int32, (tq, tq), 1)
    b0 = jnp.where(r >= c, b0, NEG)
    b0 = jnp.concatenate([b0, b0], axis=0)
    causal = b0 > 0.5 * NEG
    m, l, acc = update(carry, i, lambda s: jnp.where(causal, s + b0, NEG))
    o = acc / l
    o = o[:tq] - lam_ref[0] * o[tq:]
    o_ref[0] = (_rms(o, g_ref[...]) * out_scale).astype(BF16)


def _prompt_attn(lam, rel_bias, qab, kab, vab, tiles, subln, tq, out_scale):
    b, t, _ = qab.shape
    blk = lambda bb, h, i: (bb, i, h)
    full = lambda bb, h, i: (bb, 0, h)
    return pl.pallas_call(
        functools.partial(_prompt_attn_kernel, tq=tq, out_scale=out_scale),
        grid=(b, H_A, t // tq),
        in_specs=[pl.BlockSpec(memory_space=pltpu.SMEM), pl.BlockSpec(memory_space=pltpu.SMEM),
                  pl.BlockSpec((1, tq, 2 * DH), blk), pl.BlockSpec((1, t, 2 * DH), full),
                  pl.BlockSpec((1, t, VD), full),
                  pl.BlockSpec((1, 2, tq, tq), lambda bb, h, i: (h, 0, 0, 0)),
                  pl.BlockSpec((1, VD), lambda bb, h, i: (0, 0))],
        out_specs=pl.BlockSpec((1, tq, VD), blk),
        out_shape=jax.ShapeDtypeStruct((b, t, H_A * VD), BF16),
        compiler_params=_cparams(("parallel", "parallel", "arbitrary")),
        name="prompt_attn",
    )(lam, rel_bias.reshape(-1), qab, kab, vab, tiles, subln)


SCOLS = 128
TPAD = 8


def _sample_attn_kernel(pt_ref, lam_ref, wq_ref, *refs, pp, n_pages, ts, out_scale):
    k_refs = refs[:pp]
    v_refs = refs[pp:2 * pp]
    (knew_ref, vnew_ref, far_ref, tbl_ref, g_ref, o_ref, m_ref, l_ref, acc_ref) = refs[2 * pp:]
    j = pl.program_id(1)
    nsteps = n_pages // pp
    past = n_pages * PAGE_SIZE

    @pl.when(j == 0)
    def _():
        m_ref[...] = jnp.full(m_ref.shape, NEG, F32)
        l_ref[...] = jnp.zeros(l_ref.shape, F32)
        acc_ref[...] = jnp.zeros(acc_ref.shape, F32)

    wq = wq_ref[0]

    def col_to_rows(row):
        t = jnp.transpose(jnp.broadcast_to(row, (SCOLS, SCOLS)))
        return jnp.concatenate([t] * H_A, axis=1)

    def update(s, vb):
        m = m_ref[...]
        m_new = jnp.maximum(m, jnp.max(s, axis=0, keepdims=True))
        alpha = jnp.exp(m - m_new)
        p = jnp.exp(s - m_new)
        l_ref[...] = alpha * l_ref[...] + jnp.sum(p, axis=0, keepdims=True)
        m_ref[...] = m_new
        pt = jnp.transpose(p).astype(BF16)
        acc_ref[...] = acc_ref[...] * col_to_rows(alpha) + jnp.dot(pt, vb, preferred_element_type=F32)

    def near_bias(rel):
        bucket = _t5_bucket(rel)
        acc = jnp.broadcast_to(tbl_ref[0:1, :], rel.shape)
        for b in range(1, NUM_BUCKETS):
            acc = jnp.where(bucket == b, tbl_ref[b:b + 1, :], acc)
        return acc

    def tok_of_col(shape):
        return lax.broadcasted_iota(jnp.int32, shape, 1) % TPAD

    far = far_ref[...]
    s_pages = [jnp.dot(k_refs[u][0, 0].astype(BF16), wq, preferred_element_type=F32) for u in range(pp)]
    v_pages = [v_refs[u][0, 0].astype(BF16) for u in range(pp)]

    @pl.when(j < nsteps - 1)
    def _():
        update(jnp.concatenate(s_pages, axis=0) + far, jnp.concatenate(v_pages, axis=0))

    @pl.when(j == nsteps - 1)
    def _():
        shape = (PAGE_SIZE, SCOLS)
        kpos = (n_pages - 1) * PAGE_SIZE + lax.broadcasted_iota(jnp.int32, shape, 0)
        rel = past + tok_of_col(shape) - kpos
        s_last = s_pages[pp - 1] + near_bias(rel)
        s_all = jnp.concatenate([sp + far for sp in s_pages[:pp - 1]] + [s_last], axis=0)
        update(s_all, jnp.concatenate(v_pages, axis=0))
        knew = knew_ref[0]
        shape = (knew.shape[0], SCOLS)
        kt = lax.broadcasted_iota(jnp.int32, shape, 0)
        rel = tok_of_col(shape) - kt
        s_new = jnp.dot(knew, wq, preferred_element_type=F32) + near_bias(rel)
        s_new = jnp.where((rel >= 0) & (kt < ts), s_new, NEG)
        update(s_new, vnew_ref[0])
        inv = col_to_rows(1.0 / l_ref[...])
        o = acc_ref[...] * inv
        lam = lam_ref[0]
        for h in range(H_A):
            lanes = slice(h * VD, (h + 1) * VD)
            r0 = h * 2 * TPAD
            oh = o[r0:r0 + TPAD, lanes] - lam * o[r0 + TPAD:r0 + 2 * TPAD, lanes]
            o_ref[0, :, lanes] = (_rms(oh, g_ref[...]) * out_scale).astype(o_ref.dtype)


def _sample_attn(layer, page_table, lam, wq, cache_k4, cache_v4, knew, vnew, far_row, tbl, subln, pp, ts,
                 out_scale):
    db, n_pages = page_table.shape
    nsteps = n_pages // pp

    def page_map(u):
        return lambda b, j, pt: (layer, pt[b * n_pages + j * pp + u], 0, 0)

    per_b = lambda b, j, pt: (b, 0, 0)
    const = lambda b, j, pt: (0, 0)
    page_spec = [pl.BlockSpec((1, 1, PAGE_SIZE, HA_COLS), page_map(u)) for u in range(pp)]
    grid_spec = pltpu.PrefetchScalarGridSpec(
        num_scalar_prefetch=1,
        grid=(db, nsteps),
        in_specs=[pl.BlockSpec(memory_space=pltpu.SMEM), pl.BlockSpec((1, HA_COLS, SCOLS), per_b)]
        + page_spec + page_spec
        + [pl.BlockSpec((1,) + knew.shape[1:], per_b), pl.BlockSpec((1,) + vnew.shape[1:], per_b),
           pl.BlockSpec((1, SCOLS), const), pl.BlockSpec((NUM_BUCKETS, SCOLS), const),
           pl.BlockSpec((1, VD), const)],
        out_specs=pl.BlockSpec((1, TPAD, H_A * VD), per_b),
        scratch_shapes=[pltpu.VMEM((1, SCOLS), F32), pltpu.VMEM((1, SCOLS), F32),
                        pltpu.VMEM((SCOLS, H_A * VD), F32)],
    )
    return pl.pallas_call(
        functools.partial(_sample_attn_kernel, pp=pp, n_pages=n_pages, ts=ts, out_scale=out_scale),
        grid_spec=grid_spec,
        out_shape=jax.ShapeDtypeStruct((db, TPAD, H_A * VD), BF16),
        compiler_params=_cparams(("parallel", "arbitrary")),
        name="sample_attn",
    )(page_table.reshape(-1), lam, wq, *([cache_k4] * pp), *([cache_v4] * pp), knew, vnew, far_row, tbl, subln)


def _hgrn_consts(chunk):
    t = np.arange(chunk)
    rows = []
    for m in HGRN_LEVELS:
        same = (t[:, None] // m) == (t[None, :] // m)
        rows.append(same & (t[None, :] <= t[:, None]))
        rows.append(same & (t[None, :] > t[:, None]))
    cum = np.concatenate(rows, axis=0).astype(np.float32)
    masks = []
    for m in HGRN_LEVELS[:-1]:
        bt, bs = t[:, None] // m, t[None, :] // m
        masks.append((bt == bs + 1) & (bs % 2 == 0))
    masks.append(t[:, None] == t[None, :])
    return jnp.asarray(cum, BF16), jnp.asarray(np.stack(masks).astype(np.float32))


def _hgrn_kernel(q_ref, f_ref, i_ref, g_ref, lb_ref, gain_ref, s0_ref, cum_ref, mask_ref,
                 o_ref, s_out_ref, st_ref, *, tblk, valid_len):
    c = HGRN_CHUNK
    nlev = len(HGRN_LEVELS)
    tstep = pl.program_id(1)

    @pl.when(tstep == 0)
    def _():
        for h in range(H_B):
            st_ref[h] = jnp.transpose(s0_ref[0, h])

    lb = lb_ref[...]
    cum = cum_ref[...]
    for ci in range(tblk // c):
        rows = slice(ci * c, (ci + 1) * c)
        f = lb + (1.0 - lb) * jax.nn.sigmoid(f_ref[0, rows, :])
        g = jnp.log(jnp.maximum(f, F_MIN))
        kh = 1.0 - f
        if valid_len is not None:
            pos = tstep * tblk + ci * c + lax.broadcasted_iota(jnp.int32, g.shape, 0)
            g = jnp.where(pos < valid_len, g, 0.0)
            kh = jnp.where(pos < valid_len, kh, 0.0)
        g_hi = g.astype(BF16)
        r1 = g - g_hi.astype(F32)
        g_mid = r1.astype(BF16)
        g_lo = (r1 - g_mid.astype(F32)).astype(BF16)
        ce = (jnp.dot(cum, g_hi, preferred_element_type=F32) + jnp.dot(cum, g_mid, preferred_element_type=F32)
              + jnp.dot(cum, g_lo, preferred_element_type=F32))
        for h in range(H_B):
            lanes = slice(h * DK, (h + 1) * DK)
            q = q_ref[0, rows, lanes]
            k = kh[:, lanes]
            vb = i_ref[0, rows, lanes].astype(BF16)
            a = jnp.where(mask_ref[nlev - 1] > 0.5, _dot_nt(q.astype(BF16), k.astype(BF16)), 0.0)
            for lv in range(nlev - 1):
                cm = ce[(2 * lv) * c:(2 * lv + 1) * c, lanes]
                em = ce[(2 * lv + 1) * c:(2 * lv + 2) * c, lanes]
                qt = (q * jnp.exp(cm)).astype(BF16)
                kt = (k * jnp.exp(em)).astype(BF16)
                a = a + jnp.where(mask_ref[lv] > 0.5, _dot_nt(qt, kt), 0.0)
            b_incl = ce[(2 * nlev - 2) * c:(2 * nlev - 1) * c, lanes]
            b_rest = ce[(2 * nlev - 1) * c:(2 * nlev) * c, lanes]
            st = st_ref[h]
            o = _dot_nt((q * jnp.exp(b_incl)).astype(BF16), st.astype(BF16))
            o = o + jnp.dot(a.astype(BF16), vb, preferred_element_type=F32)
            k_end = (k * jnp.exp(b_rest)).astype(BF16)
            vt = jnp.transpose(i_ref[0, rows, lanes]).astype(BF16)
            decay_end = jnp.exp(b_incl[c - 1:c, :])
            st_ref[h] = st * decay_end + jnp.dot(vt, k_end, preferred_element_type=F32)
            gate = g_ref[0, rows, lanes]
            o = _rms(o, gain_ref[...]) * (gate * jax.nn.sigmoid(gate))
            o_ref[0, rows, lanes] = o.astype(BF16)

    @pl.when(tstep == pl.num_programs(1) - 1)
    def _():
        for h in range(H_B):
            s_out_ref[0, h] = jnp.transpose(st_ref[h])


def _hgrn(qh, fh, ih, gh, lb, gain, s0, tblk, valid_len):
    b, t, w = qh.shape
    cum, masks = _hgrn_consts(HGRN_CHUNK)
    blk = lambda bb, i: (bb, i, 0)
    const2 = lambda bb, i: (0, 0)
    spec = pl.BlockSpec((1, tblk, w), blk)
    s_spec = pl.BlockSpec((1, H_B, DK, DV), lambda bb, i: (bb, 0, 0, 0))
    return pl.pallas_call(
        functools.partial(_hgrn_kernel, tblk=tblk, valid_len=valid_len),
        grid=(b, t // tblk),
        in_specs=[spec, spec, spec, spec, pl.BlockSpec((1, w), const2), pl.BlockSpec((1, DV), const2), s_spec,
                  pl.BlockSpec(cum.shape, const2), pl.BlockSpec(masks.shape, lambda bb, i: (0, 0, 0))],
        out_specs=[spec, s_spec],
        out_shape=[jax.ShapeDtypeStruct((b, t, w), BF16), jax.ShapeDtypeStruct((b, H_B, DK, DV), F32)],
        scratch_shapes=[pltpu.VMEM((H_B, DV, DK), F32)],
        compiler_params=_cparams(("parallel", "arbitrary")),
        name="hgrn",
    )(qh, fh, ih, gh, lb, gain, s0, cum, masks)


def _memkv_kernel(x_ref, ln_ref, wk_ref, wv_ref, gk_ref, k_ref, v_ref, kb_ref, vb_ref):
    mb = _rms(x_ref[...], ln_ref[...]).astype(BF16)
    k = jnp.dot(mb, wk_ref[...], preferred_element_type=F32)
    for h in range(MH):
        lanes = slice(h * MDH, (h + 1) * MDH)
        kn = _rms(k[:, lanes], gk_ref[...])
        k_ref[:, lanes] = kn
        kb_ref[:, lanes] = kn.astype(BF16)
    v = jnp.dot(mb, wv_ref[...], preferred_element_type=F32)
    v_ref[...] = v
    vb_ref[...] = v.astype(BF16)


def _memkv(mem2d, ln, wk_b, wv_b, gk, tm):
    n, d = mem2d.shape
    w = MH * MDH
    row = lambda i: (i, 0)
    const = lambda i: (0, 0)
    o_spec = pl.BlockSpec((tm, w), row)
    return pl.pallas_call(
        _memkv_kernel,
        grid=(n // tm,),
        in_specs=[pl.BlockSpec((tm, d), row), pl.BlockSpec((1, d), const), pl.BlockSpec((d, w), const),
                  pl.BlockSpec((d, w), const), pl.BlockSpec((1, MDH), const)],
        out_specs=[o_spec] * 4,
        out_shape=[jax.ShapeDtypeStruct((n, w), F32), jax.ShapeDtypeStruct((n, w), F32),
                   jax.ShapeDtypeStruct((n, w), BF16), jax.ShapeDtypeStruct((n, w), BF16)],
        compiler_params=_cparams(("parallel",)),
        name="memkv",
    )(mem2d, ln, wk_b, wv_b, gk)


def _mid_kernel(x_ref, oa_ref, ob_ref, woa_ref, wob_ref, ln_ref, wq_ref, gq_ref, mk_ref, mv_ref, wo_ref,
                y_ref, qm_ref, om_ref, *, bb, tt):
    x = x_ref[...]
    x = x + jnp.dot(oa_ref[...], woa_ref[...], preferred_element_type=F32) + jnp.dot(
        ob_ref[...], wob_ref[...], preferred_element_type=F32)
    hb = _rms(x, ln_ref[...]).astype(BF16)
    qm = jnp.dot(hb, wq_ref[...], preferred_element_type=F32)
    for h in range(MH):
        lanes = slice(h * MDH, (h + 1) * MDH)
        qm_ref[:, lanes] = _rms(qm[:, lanes], gq_ref[...]) * (MDH ** -0.5)

    def per_batch(bi, carry):
        r = pl.ds(pl.multiple_of(bi * tt, tt), tt)
        for h in range(MH):
            lanes = slice(h * MDH, (h + 1) * MDH)
            s = _dot_nt(qm_ref[r, lanes].astype(BF16), mk_ref[bi, :, lanes])
            s = s - jnp.max(s, axis=-1, keepdims=True)
            p = jnp.exp(s)
            p = p / jnp.sum(p, axis=-1, keepdims=True)
            om_ref[r, lanes] = jnp.dot(p.astype(BF16), mv_ref[bi, :, lanes], preferred_element_type=F32)
        return carry

    lax.fori_loop(0, bb, per_batch, 0)
    y_ref[...] = x + jnp.dot(om_ref[...].astype(BF16), wo_ref[...], preferred_element_type=F32)


def _mid(x2d, oa, ob, woa, wob, ln, wq, gq, mkb, mvb, wo, t, bb, tt):
    n, d = x2d.shape
    n_mem = mkb.shape[1]
    w = H_A * VD
    assert bb == 1 or tt == t
    per_batch = t // tt
    row = lambda i: (i, 0)
    const = lambda i: (0, 0)
    mem_spec = pl.BlockSpec((bb, n_mem, MH * MDH), lambda i: (i // per_batch, 0, 0))
    rows = bb * tt
    return pl.pallas_call(
        functools.partial(_mid_kernel, bb=bb, tt=tt),
        grid=(n // rows,),
        in_specs=[pl.BlockSpec((rows, d), row), pl.BlockSpec((rows, w), row), pl.BlockSpec((rows, w), row),
                  pl.BlockSpec((w, d), const), pl.BlockSpec((w, d), const), pl.BlockSpec((1, d), const),
                  pl.BlockSpec((d, MH * MDH), const), pl.BlockSpec((1, MDH), const), mem_spec, mem_spec,
                  pl.BlockSpec((MH * MDH, d), const)],
        out_specs=pl.BlockSpec((rows, d), row),
        out_shape=jax.ShapeDtypeStruct((n, d), F32),
        scratch_shapes=[pltpu.VMEM((rows, MH * MDH), F32), pltpu.VMEM((rows, MH * MDH), F32)],
        compiler_params=_cparams(("parallel",)),
        name="mid",
    )(x2d, oa, ob, woa, wob, ln, wq, gq, mkb, mvb, wo)


def _top16_desc(x, dst_ref):
    for r in range(PEER_TOPK):
        m = jnp.max(x, axis=0, keepdims=True)
        dst_ref[r:r + 1, :] = m
        if r + 1 < PEER_TOPK:
            x = jnp.where(x == m, -jnp.inf, x)


def _peer_kernel(x_ref, ln_ref, wqt_ref, sk_ref, u_ref, vt_ref, y_ref,
                 ht_ref, th_ref, e1_ref, s2_ref, w2_ref, acc_ref, v1_ref, v2_ref, cand_ref, *, ec):
    cidx = pl.program_id(1)
    tt = x_ref.shape[0]

    @pl.when(cidx == 0)
    def _():
        hn = _rms(x_ref[...], ln_ref[...])
        ht_ref[...] = jnp.transpose(hn).astype(BF16)
        acc_ref[...] = jnp.zeros(acc_ref.shape, F32)

        def per_head(hd, carry):
            def scores(p):
                r0 = pl.multiple_of((hd * 2 + p) * N_KEYS, N_KEYS)
                qt = jnp.dot(wqt_ref[pl.ds(r0, N_KEYS), :], ht_ref[...], preferred_element_type=F32)
                return jnp.dot(sk_ref[hd * 2 + p], qt.astype(BF16), preferred_element_type=F32)

            s1 = scores(0)
            s2 = scores(1)
            _top16_desc(s1, v1_ref)
            _top16_desc(s2, v2_ref)
            cand_ref[0:16, :] = v1_ref[0:1, :] + v2_ref[...]
            for i in range(1, 8):
                cand_ref[8 + 8 * i:16 + 8 * i, :] = v1_ref[i:i + 1, :] + v2_ref[0:8, :]
            cand_ref[72:80, :] = v1_ref[8:16, :] + v2_ref[0:1, :]
            cand = cand_ref[...]
            x = cand
            for r in range(PEER_TOPK - 1):
                x = jnp.where(x == jnp.max(x, axis=0, keepdims=True), -jnp.inf, x)
            tau = jnp.max(x, axis=0, keepdims=True)
            top = v1_ref[0:1, :] + v2_ref[0:1, :]
            z = jnp.sum(jnp.where(cand >= tau, jnp.exp(cand - top), 0.0), axis=0, keepdims=True)
            th_ref[hd] = tau - s1
            e1_ref[hd] = jnp.exp(s1 - v1_ref[0:1, :])
            s2_ref[hd] = s2
            w2_ref[hd] = jnp.exp(s2 - v2_ref[0:1, :]) / z
            return carry

        lax.fori_loop(0, PEER_HEADS, per_head, 0)

    act = jnp.dot(u_ref[...], ht_ref[...], preferred_element_type=F32)
    act = 0.5 * act * (1.0 + lax.erf(act * SQRT_HALF))
    n_a = ec // N_KEYS
    blocks = []
    for aa in range(n_a):
        a = cidx * n_a + aa
        g = jnp.zeros((N_KEYS, tt), F32)
        for hd in range(PEER_HEADS):
            th = th_ref[hd, pl.ds(a, 1), :]
            e1 = e1_ref[hd, pl.ds(a, 1), :]
            g = g + jnp.where(s2_ref[hd] >= th, w2_ref[hd] * e1, 0.0)
        blocks.append(g * act[aa * N_KEYS:(aa + 1) * N_KEYS, :])
    w = jnp.concatenate(blocks, axis=0).astype(BF16)
    acc_ref[...] += jnp.dot(vt_ref[...], w, preferred_element_type=F32)

    @pl.when(cidx == pl.num_programs(1) - 1)
    def _():
        y_ref[...] = x_ref[...] + jnp.transpose(acc_ref[...])


def _peer(x2d, ln, wqt_b, sk_b, u_b, vt_b, tt, ec):
    n, d = x2d.shape
    n_exp = u_b.shape[0]
    row = lambda i, c: (i, 0)
    const = lambda i, c: (0, 0)
    return pl.pallas_call(
        functools.partial(_peer_kernel, ec=ec),
        grid=(n // tt, n_exp // ec),
        in_specs=[pl.BlockSpec((tt, d), row), pl.BlockSpec((1, d), const), pl.BlockSpec(wqt_b.shape, const),
                  pl.BlockSpec(sk_b.shape, lambda i, c: (0, 0, 0)),
                  pl.BlockSpec((ec, d), lambda i, c: (c, 0)), pl.BlockSpec((d, ec), lambda i, c: (0, c))],
        out_specs=pl.BlockSpec((tt, d), row),
        out_shape=jax.ShapeDtypeStruct((n, d), F32),
        scratch_shapes=[pltpu.VMEM((d, tt), BF16)]
        + [pltpu.VMEM((PEER_HEADS, N_KEYS, tt), F32)] * 4
        + [pltpu.VMEM((d, tt), F32), pltpu.VMEM((PEER_TOPK, tt), F32), pltpu.VMEM((PEER_TOPK, tt), F32),
           pltpu.VMEM((80, tt), F32)],
        compiler_params=_cparams(("parallel", "arbitrary")),
        name="peer",
    )(x2d, ln, wqt_b, sk_b, u_b, vt_b)


def _tile(n, pref):
    t = min(n, pref)
    while n % t:
        t //= 2
    return t


def _pad_axis(a, axis, size):
    pad = [(0, 0)] * a.ndim
    pad[axis] = (0, size - a.shape[axis])
    return jnp.pad(a, pad)


def kernel(x_prompt, x_sample, cache_k, cache_v, cache_mem_k, cache_mem_v, state_hgrn, page_table, mem_prompt,
           ln_mix, w_in, qk_gain, lambda_qk, subln_gain, rel_bias, lower_bounds, hgrn_norm, w_out, ln_mem,
           mem_norm, w_mq, w_mk, w_mv, mem_qk_gain, w_mo, ln_ffn, peer_wq, peer_subkeys, peer_u, peer_v):
    B, T, D = x_prompt.shape
    DB, TS, _ = x_sample.shape
    depth = ln_mix.shape[0]
    n_pages = page_table.shape[1]
    n_mem = mem_prompt.shape[1]
    n_phys = cache_k.shape[1]
    W = HA_COLS

    lbp = jax.nn.softmax(lower_bounds.astype(F32), axis=0)
    lbs = jnp.cumsum(lbp, axis=0) - lbp[0]
    p64 = jnp.asarray(np.kron(np.eye(W // DH), np.full((DH, DH), 1.0 / DH)), BF16)
    tq = _tile(T, 256)
    tiles = _bias_tiles(rel_bias, tq)
    cache_k4 = cache_k.reshape(depth, n_phys, PAGE_SIZE, W)
    cache_v4 = cache_v.reshape(depth, n_phys, PAGE_SIZE, W)
    col_head = np.minimum(np.arange(SCOLS) // (2 * TPAD), H_A - 1)
    col_live = (np.arange(SCOLS) < H_A * 2 * TPAD).astype(np.float32)
    tbl = rel_bias[:, col_head] * col_live[None, :]
    far_row = tbl[NUM_BUCKETS - 1:NUM_BUCKETS]
    row_grp = np.arange(W) // DH
    col_grp = np.arange(SCOLS) // TPAD
    wq_mask = jnp.asarray((row_grp[:, None] == col_grp[None, :]).astype(np.float32), BF16)
    col_tok = np.arange(SCOLS) % TPAD

    ts_pad = 8
    yp, ys = x_prompt, x_sample
    outs = {k: [] for k in ("kp", "vp", "mkp", "mvp", "sp", "ks", "vs", "ss")}
    for l in range(depth):
        lambda_init = 0.8 - 0.6 * math.exp(-0.3 * l)
        out_scale = 1.0 - lambda_init
        lq1, lk1, lq2, lk2 = lambda_qk[l].astype(F32)
        lam = (jnp.exp(jnp.sum(lq1 * lk1)) - jnp.exp(jnp.sum(lq2 * lk2)) + lambda_init).reshape(1)
        w_in_b = w_in[l].astype(BF16)
        gq = jnp.tile(qk_gain[l, 0], W // DH).reshape(1, W)
        gk = jnp.tile(qk_gain[l, 1], W // DH).reshape(1, W)
        ln_mix_l = ln_mix[l].reshape(1, D)
        subln = subln_gain[l].reshape(1, VD)
        lb = lbs[l].reshape(1, H_B * DK)
        hgain = hgrn_norm[l].reshape(1, DV)
        woa = w_out[l, :H_A * VD].astype(BF16)
        wob = w_out[l, H_A * VD:].astype(BF16)
        ln_mem_l = ln_mem[l].reshape(1, D)
        wmq = w_mq[l].astype(BF16)
        gmq = mem_qk_gain[l, 0].reshape(1, MDH)
        gmk = mem_qk_gain[l, 1].reshape(1, MDH)
        wmo = w_mo[l].astype(BF16)
        ln_ffn_l = ln_ffn[l].reshape(1, D)
        wqt = peer_wq[l].T.astype(BF16)
        sk = peer_subkeys[l].reshape(PEER_HEADS * 2, N_KEYS, -1).astype(BF16)
        u_b = peer_u[l].astype(BF16)
        vt_b = peer_v[l].T.astype(BF16)

        mk, mv, mkb, mvb = _memkv(mem_prompt.reshape(B * n_mem, D), mem_norm[l].reshape(1, D),
                                  w_mk[l].astype(BF16), w_mv[l].astype(BF16), gmk, _tile(B * n_mem, 512))
        qab, ka, va, kab, vab, qh, fh, ih, gh = _in_proj(yp.reshape(B * T, D), ln_mix_l, w_in_b, gq, gk, p64,
                                                         _tile(B * T, 512))
        r3 = lambda a: a.reshape(B, T, W)
        oa = _prompt_attn(lam, rel_bias, r3(qab), r3(kab), r3(vab), tiles, subln, tq, out_scale)
        ob, sp = _hgrn(r3(qh), r3(fh), r3(ih), r3(gh), lb, hgain, jnp.zeros((B, H_B, DK, DV), F32),
                       _tile(T, 256), None)
        x2 = _mid(yp.reshape(B * T, D), oa.reshape(B * T, W), ob.reshape(B * T, W), woa, wob, ln_mem_l, wmq, gmq,
                  mkb.reshape(B, n_mem, W), mvb.reshape(B, n_mem, W), wmo, T, 1, _tile(T, 512))
        yp = _peer(x2, ln_ffn_l, wqt, sk, u_b, vt_b, _tile(B * T, 512), 512).reshape(B, T, D)
        outs["kp"].append(ka.reshape(B, T, H_A, 2, DH))
        outs["vp"].append(va.reshape(B, T, H_A, VD))
        outs["mkp"].append(mk.reshape(B, n_mem, MH, MDH))
        outs["mvp"].append(mv.reshape(B, n_mem, MH, MDH))
        outs["sp"].append(sp)

        qab, ka, va, kab, vab, qh, fh, ih, gh = _in_proj(ys.reshape(DB * TS, D), ln_mix_l, w_in_b, gq, gk, p64,
                                                         _tile(DB * TS, 512))
        s3 = lambda a: a.reshape(DB, TS, W)
        qs = _pad_axis(s3(qab), 1, TPAD)
        wq_s = jnp.swapaxes(qs, 1, 2)[:, :, col_tok] * wq_mask[None]
        knew = _pad_axis(s3(kab), 1, 16)
        vnew = _pad_axis(s3(vab), 1, 16)
        oa_s = _sample_attn(l, page_table, lam, wq_s, cache_k4, cache_v4, knew, vnew, far_row, tbl, subln,
                            _tile(n_pages, 8), TS, out_scale)
        hp = lambda a: _pad_axis(s3(a), 1, HGRN_CHUNK)
        ob_s, ss = _hgrn(hp(qh), hp(fh), hp(ih), hp(gh), lb, hgain, state_hgrn[l].astype(F32), HGRN_CHUNK, TS)
        x2 = _mid(_pad_axis(ys, 1, ts_pad).reshape(DB * ts_pad, D), oa_s.reshape(DB * ts_pad, W),
                  ob_s[:, :ts_pad].reshape(DB * ts_pad, W), woa, wob, ln_mem_l, wmq, gmq,
                  cache_mem_k[l].reshape(DB, n_mem, W).astype(BF16),
                  cache_mem_v[l].reshape(DB, n_mem, W).astype(BF16), wmo, ts_pad, DB, ts_pad)
        ys = _peer(x2, ln_ffn_l, wqt, sk, u_b, vt_b, DB * ts_pad, 512).reshape(DB, ts_pad, D)[:, :TS]
        outs["ks"].append(ka.reshape(DB, TS, H_A, 2, DH))
        outs["vs"].append(va.reshape(DB, TS, H_A, VD))
        outs["ss"].append(ss.astype(state_hgrn.dtype))

    st = lambda k: jnp.stack(outs[k])
    return (yp, ys, st("kp"), st("vp"), st("mkp"), st("mvp"), st("sp"), st("ks"), st("vs"), st("ss"))
```

```python
import functools
import math

import numpy as np
import jax
import jax.numpy as jnp
from jax import lax
from jax.experimental import pallas as pl
from jax.experimental.pallas import tpu as pltpu

F32 = jnp.float32
BF16 = jnp.bfloat16

PAGE_SIZE = 128
H_A = 4
DH = 64
VD = 2 * DH
H_B = 4
DK = 128
DV = 128
NUM_BUCKETS = 32
MAX_DISTANCE = 128
MH = 4
MDH = 128
PEER_HEADS = 8
N_KEYS = 128
PEER_TOPK = 16
EPS = 1e-6
NEG = -1e30
F_MIN = 1e-20
HA_COLS = H_A * 2 * DH
SQRT_HALF = 0.7071067811865476

VMEM_LIMIT_BYTES = 56 * 1024 * 1024
HGRN_CHUNK = 64
HGRN_LEVELS = (1, 2, 4, 8, 16, 32, 64)
PEER_EC = 1024
PEER_PIECES = 4


def _cparams(sem):
    return pltpu.CompilerParams(dimension_semantics=sem, vmem_limit_bytes=VMEM_LIMIT_BYTES)


def _rms(x, g):
    return x * lax.rsqrt(jnp.mean(x * x, axis=-1, keepdims=True) + EPS) * g


def _group_mean_sq(x, p_ref):
    sq = x * x
    hi = sq.astype(BF16)
    lo = (sq - hi.astype(F32)).astype(BF16)
    p = p_ref[...]
    return jnp.dot(hi, p, preferred_element_type=F32) + jnp.dot(lo, p, preferred_element_type=F32)


def _dot_nt(a, b):
    return lax.dot_general(a, b, (((1,), (1,)), ((), ())), preferred_element_type=F32)


def _t5_bucket(rel):
    n = jnp.maximum(rel, 0)
    max_exact = NUM_BUCKETS // 2
    nf = jnp.maximum(n, max_exact).astype(F32)
    large = max_exact + (jnp.log(nf / max_exact) / math.log(MAX_DISTANCE / max_exact)
                         * (NUM_BUCKETS - max_exact)).astype(jnp.int32)
    large = jnp.minimum(large, NUM_BUCKETS - 1)
    return jnp.where(n < max_exact, n, large)


def _in_proj_kernel(x_ref, ln_ref, w_ref, gq_ref, gk_ref, p_ref,
                    qab_ref, ka_ref, va_ref, kab_ref, vab_ref, qh_ref, fh_ref, ih_ref, gh_ref):
    hb = _rms(x_ref[...], ln_ref[...]).astype(BF16)

    def proj(j):
        return jnp.dot(hb, w_ref[:, j * HA_COLS:(j + 1) * HA_COLS], preferred_element_type=F32)

    qa = proj(0)
    qn = qa * lax.rsqrt(_group_mean_sq(qa, p_ref) + EPS) * gq_ref[...]
    qab_ref[...] = (qn * (DH ** -0.5)).astype(BF16)
    ka = proj(1)
    kn = ka * lax.rsqrt(_group_mean_sq(ka, p_ref) + EPS) * gk_ref[...]
    ka_ref[...] = kn
    kab_ref[...] = kn.astype(BF16)
    va = proj(2)
    va_ref[...] = va
    vab_ref[...] = va.astype(BF16)
    qh_ref[...] = proj(3)
    fh_ref[...] = proj(4)
    ih_ref[...] = proj(5)
    gh_ref[...] = proj(6)


def _in_proj(x2d, ln, w_b, gq, gk, p64, tm):
    n, d = x2d.shape
    cols = w_b.shape[1]
    row = lambda i: (i, 0)
    const = lambda i: (0, 0)
    o_spec = pl.BlockSpec((tm, HA_COLS), row)
    f32o = jax.ShapeDtypeStruct((n, HA_COLS), F32)
    b16o = jax.ShapeDtypeStruct((n, HA_COLS), BF16)
    return pl.pallas_call(
        _in_proj_kernel,
        grid=(n // tm,),
        in_specs=[pl.BlockSpec((tm, d), row), pl.BlockSpec((1, d), const), pl.BlockSpec((d, cols), const),
                  pl.BlockSpec((1, HA_COLS), const), pl.BlockSpec((1, HA_COLS), const),
                  pl.BlockSpec((HA_COLS, HA_COLS), const)],
        out_specs=[o_spec] * 9,
        out_shape=[b16o, f32o, f32o, b16o, b16o, f32o, f32o, f32o, f32o],
        compiler_params=_cparams(("parallel",)),
        name="in_proj",
    )(x2d, ln, w_b, gq, gk, p64)


def _bias_tile_kernel(rb_ref, o_ref, *, tb):
    h = pl.program_id(0)
    d = pl.program_id(1)
    r = lax.broadcasted_iota(jnp.int32, (tb, tb), 0)
    c = lax.broadcasted_iota(jnp.int32, (tb, tb), 1)
    bucket = _t5_bucket(d * tb + r - c)
    acc = jnp.full((tb, tb), rb_ref[h], F32)
    for b in range(1, NUM_BUCKETS):
        acc = jnp.where(bucket == b, rb_ref[b * H_A + h], acc)
    o_ref[0, 0] = acc


def _bias_tiles(rel_bias, tb):
    return pl.pallas_call(
        functools.partial(_bias_tile_kernel, tb=tb),
        grid=(H_A, 2),
        in_specs=[pl.BlockSpec(memory_space=pltpu.SMEM)],
        out_specs=pl.BlockSpec((1, 1, tb, tb), lambda h, d: (h, d, 0, 0)),
        out_shape=jax.ShapeDtypeStruct((H_A, 2, tb, tb), F32),
        compiler_params=_cparams(("parallel", "parallel")),
        name="bias_tiles",
    )(rel_bias.reshape(-1))


def _prompt_attn_kernel(lam_ref, rb_ref, q_ref, k_ref, v_ref, bias_ref, g_ref, o_ref, *, tq, out_scale):
    h = pl.program_id(1)
    i = pl.program_id(2)
    q = q_ref[0]
    lane = lax.broadcasted_iota(jnp.int32, (tq, 2 * DH), 1)
    zero = jnp.zeros_like(q)
    q2 = jnp.concatenate([jnp.where(lane < DH, q, zero), jnp.where(lane >= DH, q, zero)], axis=0)
    far_bias = rb_ref[(NUM_BUCKETS - 1) * H_A + h]

    def update(carry, j, s_fn):
        m, l, acc = carry
        kj = k_ref[0, pl.ds(pl.multiple_of(j * tq, tq), tq), :]
        vj = v_ref[0, pl.ds(pl.multiple_of(j * tq, tq), tq), :]
        s = s_fn(_dot_nt(q2, kj))
        m_new = jnp.maximum(m, jnp.max(s, axis=-1, keepdims=True))
        alpha = jnp.exp(m - m_new)
        p = jnp.exp(s - m_new)
        l = alpha * l + jnp.sum(p, axis=-1, keepdims=True)
        acc = alpha * acc + jnp.dot(p.astype(BF16), vj, preferred_element_type=F32)
        return m_new, l, acc

    init = (jnp.full((2 * tq, 1), NEG, F32), jnp.zeros((2 * tq, 1), F32), jnp.zeros((2 * tq, VD), F32))
    carry = lax.fori_loop(0, jnp.maximum(i - 1, 0),
                          lambda j, c: update(c, j, lambda s: s + far_bias), init)
    b1 = bias_ref[0, 1]
    b1 = jnp.concatenate([b1, b1], axis=0)
    carry = update(carry, jnp.maximum(i - 1, 0), lambda s: jnp.where(i >= 1, s + b1, NEG))
    b0 = bias_ref[0, 0]
    r = lax.broadcasted_iota(jnp.int32, (tq, tq), 0)
    c = lax.broadcasted_iota(jnp.int32, (tq, tq), 1)
    b0 = jnp.where(r >= c, b0, NEG)
    b0 = jnp.concatenate([b0, b0], axis=0)
    causal = b0 > 0.5 * NEG
    m, l, acc = update(carry, i, lambda s: jnp.where(causal, s + b0, NEG))
    o = acc / l
    o = o[:tq] - lam_ref[0] * o[tq:]
    o_ref[0] = (_rms(o, g_ref[...]) * out_scale).astype(BF16)


def _prompt_attn(lam, rel_bias, qab, kab, vab, tiles, subln, tq, out_scale):
    b, t, _ = qab.shape
    blk = lambda bb, h, i: (bb, i, h)
    full = lambda bb, h, i: (bb, 0, h)
    return pl.pallas_call(
        functools.partial(_prompt_attn_kernel, tq=tq, out_scale=out_scale),
        grid=(b, H_A, t // tq),
        in_specs=[pl.BlockSpec(memory_space=pltpu.SMEM), pl.BlockSpec(memory_space=pltpu.SMEM),
                  pl.BlockSpec((1, tq, 2 * DH), blk), pl.BlockSpec((1, t, 2 * DH), full),
                  pl.BlockSpec((1, t, VD), full),
                  pl.BlockSpec((1, 2, tq, tq), lambda bb, h, i: (h, 0, 0, 0)),
                  pl.BlockSpec((1, VD), lambda bb, h, i: (0, 0))],
        out_specs=pl.BlockSpec((1, tq, VD), blk),
        out_shape=jax.ShapeDtypeStruct((b, t, H_A * VD), BF16),
        compiler_params=_cparams(("parallel", "parallel", "arbitrary")),
        name="prompt_attn",
    )(lam, rel_bias.reshape(-1), qab, kab, vab, tiles, subln)


TPAD = 8
SROWS = H_A * 2 * TPAD
NEW_PAD = 128


def _sample_attn_kernel(pt_ref, lam_ref, wq_ref, *refs, pp, n_pages, ts, out_scale):
    k_refs = refs[:pp]
    v_refs = refs[pp:2 * pp]
    (knew_ref, vnew_ref, far_ref, tbl_ref, g_ref, o_ref, m_ref, l_ref, acc_ref) = refs[2 * pp:]
    j = pl.program_id(1)
    nsteps = n_pages // pp
    past = n_pages * PAGE_SIZE

    @pl.when(j == 0)
    def _():
        m_ref[...] = jnp.full(m_ref.shape, NEG, F32)
        l_ref[...] = jnp.zeros(l_ref.shape, F32)
        acc_ref[...] = jnp.zeros(acc_ref.shape, F32)

    wq = wq_ref[0]

    def update(s, v_heads):
        m = m_ref[...]
        m_new = jnp.maximum(m, jnp.max(s, axis=1, keepdims=True))
        alpha = jnp.exp(m - m_new)
        p = jnp.exp(s - m_new)
        l_ref[...] = alpha * l_ref[...] + jnp.sum(p, axis=1, keepdims=True)
        m_ref[...] = m_new
        pb = p.astype(BF16)
        for h in range(H_A):
            rows = slice(h * 2 * TPAD, (h + 1) * 2 * TPAD)
            acc_ref[rows, :] = acc_ref[rows, :] * alpha[rows] + jnp.dot(pb[rows], v_heads[h],
                                                                        preferred_element_type=F32)

    def near_bias(rel):
        bucket = _t5_bucket(rel)
        acc = jnp.broadcast_to(tbl_ref[:, 0:1], rel.shape)
        for b in range(1, NUM_BUCKETS):
            acc = jnp.where(bucket == b, tbl_ref[:, b:b + 1], acc)
        return acc

    def tok_of_row(shape):
        return lax.broadcasted_iota(jnp.int32, shape, 0) % TPAD

    far = far_ref[...]
    s_pages = [jnp.dot(wq, k_refs[u][0, 0].astype(BF16), preferred_element_type=F32) for u in range(pp)]
    v_heads = [jnp.concatenate([v_refs[u][0, 0, pl.ds(h, PAGE_SIZE, stride=H_A), :].astype(BF16)
                                for u in range(pp)], axis=0) for h in range(H_A)]

    @pl.when(j < nsteps - 1)
    def _():
        update(jnp.concatenate(s_pages, axis=1) + far, v_heads)

    @pl.when(j == nsteps - 1)
    def _():
        shape = (SROWS, PAGE_SIZE)
        kpos = (n_pages - 1) * PAGE_SIZE + lax.broadcasted_iota(jnp.int32, shape, 1)
        rel = past + tok_of_row(shape) - kpos
        s_last = s_pages[pp - 1] + near_bias(rel)
        update(jnp.concatenate([sp + far for sp in s_pages[:pp - 1]] + [s_last], axis=1), v_heads)
        shape = (SROWS, NEW_PAD)
        kt = lax.broadcasted_iota(jnp.int32, shape, 1)
        rel = tok_of_row(shape) - kt
        s_new = jnp.dot(wq, knew_ref[0], preferred_element_type=F32) + near_bias(rel)
        s_new = jnp.where((rel >= 0) & (kt < ts), s_new, NEG)
        update(s_new, [vnew_ref[0, :, h * VD:(h + 1) * VD] for h in range(H_A)])
        o = acc_ref[...] / l_ref[...]
        lam = lam_ref[0]
        for h in range(H_A):
            r0 = h * 2 * TPAD
            oh = o[r0:r0 + TPAD] - lam * o[r0 + TPAD:r0 + 2 * TPAD]
            o_ref[0, :, h * VD:(h + 1) * VD] = (_rms(oh, g_ref[...]) * out_scale).astype(o_ref.dtype)


def _sample_attn(layer, page_table, lam, wq, cache_kt, cache_v2, knew_t, vnew, far_col, tbl, subln, pp, ts,
                 out_scale):
    db, n_pages = page_table.shape
    nsteps = n_pages // pp

    def page_map(u):
        return lambda b, j, pt: (layer, pt[b * n_pages + j * pp + u], 0, 0)

    per_b = lambda b, j, pt: (b, 0, 0)
    const = lambda b, j, pt: (0, 0)
    page_spec = [pl.BlockSpec((1, 1, HA_COLS, PAGE_SIZE), page_map(u)) for u in range(pp)]
    grid_spec = pltpu.PrefetchScalarGridSpec(
        num_scalar_prefetch=1,
        grid=(db, nsteps),
        in_specs=[pl.BlockSpec(memory_space=pltpu.SMEM), pl.BlockSpec((1, SROWS, HA_COLS), per_b)]
        + page_spec + page_spec
        + [pl.BlockSpec((1, HA_COLS, NEW_PAD), per_b), pl.BlockSpec((1, NEW_PAD, H_A * VD), per_b),
           pl.BlockSpec((SROWS, 1), const), pl.BlockSpec((SROWS, NUM_BUCKETS), const),
           pl.BlockSpec((1, VD), const)],
        out_specs=pl.BlockSpec((1, TPAD, H_A * VD), per_b),
        scratch_shapes=[pltpu.VMEM((SROWS, 1), F32), pltpu.VMEM((SROWS, 1), F32), pltpu.VMEM((SROWS, VD), F32)],
    )
    return pl.pallas_call(
        functools.partial(_sample_attn_kernel, pp=pp, n_pages=n_pages, ts=ts, out_scale=out_scale),
        grid_spec=grid_spec,
        out_shape=jax.ShapeDtypeStruct((db, TPAD, H_A * VD), BF16),
        compiler_params=_cparams(("parallel", "arbitrary")),
        name="sample_attn",
    )(page_table.reshape(-1), lam, wq, *([cache_kt] * pp), *([cache_v2] * pp), knew_t, vnew, far_col, tbl, subln)


def _hgrn_consts(chunk):
    t = np.arange(chunk)
    rows = []
    for m in HGRN_LEVELS:
        same = (t[:, None] // m) == (t[None, :] // m)
        rows.append(same & (t[None, :] <= t[:, None]))
        rows.append(same & (t[None, :] > t[:, None]))
    cum = np.concatenate(rows, axis=0).astype(np.float32)
    masks = []
    for m in HGRN_LEVELS[:-1]:
        bt, bs = t[:, None] // m, t[None, :] // m
        masks.append((bt == bs + 1) & (bs % 2 == 0))
    masks.append(t[:, None] == t[None, :])
    return jnp.asarray(cum, BF16), jnp.asarray(np.stack(masks).astype(np.float32))


def _hgrn_kernel(q_ref, f_ref, i_ref, g_ref, lb_ref, gain_ref, s0_ref, cum_ref, mask_ref,
                 o_ref, s_out_ref, st_ref, *, tblk, valid_len):
    c = HGRN_CHUNK
    nlev = len(HGRN_LEVELS)
    tstep = pl.program_id(1)

    @pl.when(tstep == 0)
    def _():
        for h in range(H_B):
            st_ref[h] = jnp.transpose(s0_ref[0, h])

    lb = lb_ref[...]
    cum = cum_ref[...]
    for ci in range(tblk // c):
        rows = slice(ci * c, (ci + 1) * c)
        f = lb + (1.0 - lb) * jax.nn.sigmoid(f_ref[0, rows, :])
        g = jnp.log(jnp.maximum(f, F_MIN))
        kh = 1.0 - f
        if valid_len is not None:
            pos = tstep * tblk + ci * c + lax.broadcasted_iota(jnp.int32, g.shape, 0)
            g = jnp.where(pos < valid_len, g, 0.0)
            kh = jnp.where(pos < valid_len, kh, 0.0)
        g_hi = g.astype(BF16)
        r1 = g - g_hi.astype(F32)
        g_mid = r1.astype(BF16)
        g_lo = (r1 - g_mid.astype(F32)).astype(BF16)
        ce = (jnp.dot(cum, g_hi, preferred_element_type=F32) + jnp.dot(cum, g_mid, preferred_element_type=F32)
              + jnp.dot(cum, g_lo, preferred_element_type=F32))
        for h in range(H_B):
            lanes = slice(h * DK, (h + 1) * DK)
            q = q_ref[0, rows, lanes]
            k = kh[:, lanes]
            vb = i_ref[0, rows, lanes].astype(BF16)
            a = jnp.where(mask_ref[nlev - 1] > 0.5, _dot_nt(q.astype(BF16), k.astype(BF16)), 0.0)
            for lv in range(nlev - 1):
                cm = ce[(2 * lv) * c:(2 * lv + 1) * c, lanes]
                em = ce[(2 * lv + 1) * c:(2 * lv + 2) * c, lanes]
                qt = (q * jnp.exp(cm)).astype(BF16)
                kt = (k * jnp.exp(em)).astype(BF16)
                a = a + jnp.where(mask_ref[lv] > 0.5, _dot_nt(qt, kt), 0.0)
            b_incl = ce[(2 * nlev - 2) * c:(2 * nlev - 1) * c, lanes]
            b_rest = ce[(2 * nlev - 1) * c:(2 * nlev) * c, lanes]
            st = st_ref[h]
            o = _dot_nt((q * jnp.exp(b_incl)).astype(BF16), st.astype(BF16))
            o = o + jnp.dot(a.astype(BF16), vb, preferred_element_type=F32)
            k_end = (k * jnp.exp(b_rest)).astype(BF16)
            vt = jnp.transpose(i_ref[0, rows, lanes]).astype(BF16)
            decay_end = jnp.exp(b_incl[c - 1:c, :])
            st_ref[h] = st * decay_end + jnp.dot(vt, k_end, preferred_element_type=F32)
            gate = g_ref[0, rows, lanes]
            o = _rms(o, gain_ref[...]) * (gate * jax.nn.sigmoid(gate))
            o_ref[0, rows, lanes] = o.astype(BF16)

    @pl.when(tstep == pl.num_programs(1) - 1)
    def _():
        for h in range(H_B):
            s_out_ref[0, h] = jnp.transpose(st_ref[h])


def _hgrn(qh, fh, ih, gh, lb, gain, s0, tblk, valid_len):
    b, t, w = qh.shape
    cum, masks = _hgrn_consts(HGRN_CHUNK)
    blk = lambda bb, i: (bb, i, 0)
    const2 = lambda bb, i: (0, 0)
    spec = pl.BlockSpec((1, tblk, w), blk)
    s_spec = pl.BlockSpec((1, H_B, DK, DV), lambda bb, i: (bb, 0, 0, 0))
    return pl.pallas_call(
        functools.partial(_hgrn_kernel, tblk=tblk, valid_len=valid_len),
        grid=(b, t // tblk),
        in_specs=[spec, spec, spec, spec, pl.BlockSpec((1, w), const2), pl.BlockSpec((1, DV), const2), s_spec,
                  pl.BlockSpec(cum.shape, const2), pl.BlockSpec(masks.shape, lambda bb, i: (0, 0, 0))],
        out_specs=[spec, s_spec],
        out_shape=[jax.ShapeDtypeStruct((b, t, w), BF16), jax.ShapeDtypeStruct((b, H_B, DK, DV), F32)],
        scratch_shapes=[pltpu.VMEM((H_B, DV, DK), F32)],
        compiler_params=_cparams(("parallel", "arbitrary")),
        name="hgrn",
    )(qh, fh, ih, gh, lb, gain, s0, cum, masks)


def _memkv_kernel(x_ref, ln_ref, wk_ref, wv_ref, gk_ref, k_ref, v_ref, kb_ref, vb_ref):
    mb = _rms(x_ref[...], ln_ref[...]).astype(BF16)
    k = jnp.dot(mb, wk_ref[...], preferred_element_type=F32)
    for h in range(MH):
        lanes = slice(h * MDH, (h + 1) * MDH)
        kn = _rms(k[:, lanes], gk_ref[...])
        k_ref[:, lanes] = kn
        kb_ref[:, lanes] = kn.astype(BF16)
    v = jnp.dot(mb, wv_ref[...], preferred_element_type=F32)
    v_ref[...] = v
    vb_ref[...] = v.astype(BF16)


def _memkv(mem2d, ln, wk_b, wv_b, gk, tm):
    n, d = mem2d.shape
    w = MH * MDH
    row = lambda i: (i, 0)
    const = lambda i: (0, 0)
    o_spec = pl.BlockSpec((tm, w), row)
    return pl.pallas_call(
        _memkv_kernel,
        grid=(n // tm,),
        in_specs=[pl.BlockSpec((tm, d), row), pl.BlockSpec((1, d), const), pl.BlockSpec((d, w), const),
                  pl.BlockSpec((d, w), const), pl.BlockSpec((1, MDH), const)],
        out_specs=[o_spec] * 4,
        out_shape=[jax.ShapeDtypeStruct((n, w), F32), jax.ShapeDtypeStruct((n, w), F32),
                   jax.ShapeDtypeStruct((n, w), BF16), jax.ShapeDtypeStruct((n, w), BF16)],
        compiler_params=_cparams(("parallel",)),
        name="memkv",
    )(mem2d, ln, wk_b, wv_b, gk)


def _mid_kernel(x_ref, oa_ref, ob_ref, woa_ref, wob_ref, ln_ref, wq_ref, gq_ref, mk_ref, mv_ref, wo_ref,
                y_ref, qm_ref, om_ref, *, bb, tt):
    x = x_ref[...]
    x = x + jnp.dot(oa_ref[...], woa_ref[...], preferred_element_type=F32) + jnp.dot(
        ob_ref[...], wob_ref[...], preferred_element_type=F32)
    hb = _rms(x, ln_ref[...]).astype(BF16)
    qm = jnp.dot(hb, wq_ref[...], preferred_element_type=F32)
    for h in range(MH):
        lanes = slice(h * MDH, (h + 1) * MDH)
        qm_ref[:, lanes] = _rms(qm[:, lanes], gq_ref[...]) * (MDH ** -0.5)

    def per_batch(bi, carry):
        r = pl.ds(pl.multiple_of(bi * tt, tt), tt)
        for h in range(MH):
            lanes = slice(h * MDH, (h + 1) * MDH)
            s = _dot_nt(qm_ref[r, lanes].astype(BF16), mk_ref[bi, :, lanes])
            s = s - jnp.max(s, axis=-1, keepdims=True)
            p = jnp.exp(s)
            p = p / jnp.sum(p, axis=-1, keepdims=True)
            om_ref[r, lanes] = jnp.dot(p.astype(BF16), mv_ref[bi, :, lanes], preferred_element_type=F32)
        return carry

    lax.fori_loop(0, bb, per_batch, 0)
    y_ref[...] = x + jnp.dot(om_ref[...].astype(BF16), wo_ref[...], preferred_element_type=F32)


def _mid(x2d, oa, ob, woa, wob, ln, wq, gq, mkb, mvb, wo, t, bb, tt):
    n, d = x2d.shape
    n_mem = mkb.shape[1]
    w = H_A * VD
    assert bb == 1 or tt == t
    per_batch = t // tt
    row = lambda i: (i, 0)
    const = lambda i: (0, 0)
    mem_spec = pl.BlockSpec((bb, n_mem, MH * MDH), lambda i: (i // per_batch, 0, 0))
    rows = bb * tt
    return pl.pallas_call(
        functools.partial(_mid_kernel, bb=bb, tt=tt),
        grid=(n // rows,),
        in_specs=[pl.BlockSpec((rows, d), row), pl.BlockSpec((rows, w), row), pl.BlockSpec((rows, w), row),
                  pl.BlockSpec((w, d), const), pl.BlockSpec((w, d), const), pl.BlockSpec((1, d), const),
                  pl.BlockSpec((d, MH * MDH), const), pl.BlockSpec((1, MDH), const), mem_spec, mem_spec,
                  pl.BlockSpec((MH * MDH, d), const)],
        out_specs=pl.BlockSpec((rows, d), row),
        out_shape=jax.ShapeDtypeStruct((n, d), F32),
        scratch_shapes=[pltpu.VMEM((rows, MH * MDH), F32), pltpu.VMEM((rows, MH * MDH), F32)],
        compiler_params=_cparams(("parallel",)),
        name="mid",
    )(x2d, oa, ob, woa, wob, ln, wq, gq, mkb, mvb, wo)


LANE = 128


SUBLANES = 8


def _sort_network(n):
    pairs = []
    p = 1
    while p < n:
        k = p
        while k >= 1:
            for j in range(k % p, n - k, 2 * k):
                for i in range(min(k, n - j - k)):
                    if (i + j) // (2 * p) == (i + j + k) // (2 * p):
                        pairs.append((i + j, i + j + k))
            k //= 2
        p *= 2
    return pairs


def _sorted_tiles(tiles):
    wires = list(tiles) + [None] * (PEER_TOPK - len(tiles))
    for a, b in _sort_network(PEER_TOPK):
        if wires[a] is None:
            wires[a], wires[b] = wires[b], None
        elif wires[b] is not None:
            wires[a], wires[b] = jnp.maximum(wires[a], wires[b]), jnp.minimum(wires[a], wires[b])
    return [w for w in wires if w is not None]


def _largest_of_sorted(vs, count):
    out = []
    for r in range(count):
        m = jnp.max(vs[0], axis=0, keepdims=True)
        out.append(m)
        need = count - r - 1
        if need == 0:
            break
        eq = vs[0] == m
        vs = [jnp.where(eq, vs[i + 1] if i + 1 < len(vs) else -jnp.inf, vs[i]) for i in range(min(len(vs), need))]
    return out


def _split_tiles(x):
    return [x[i * SUBLANES:(i + 1) * SUBLANES] for i in range(x.shape[0] // SUBLANES)]


def _count_leading(rows_ref, pred):
    cnt = None
    for r in range(PEER_TOPK):
        hit = pred(rows_ref[r:r + 1, :])
        cnt = jnp.where(hit, float(r + 1), 0.0 if cnt is None else cnt)
    return cnt


def _peer_kernel(x_ref, ln_ref, wqt_ref, sk_ref, u_ref, vtp_ref, vtl_ref, y_ref,
                 ht_ref, qt_ref, n_ref, e1_ref, r2_ref, w2_ref, acc_ref, s1_ref, s2_ref, v1_ref, v2_ref,
                 wa_ref, wb_ref, *, ec, n_chunks):
    cidx = pl.program_id(1)
    tt = x_ref.shape[0]

    @pl.when(cidx == 0)
    def _():
        hn = _rms(x_ref[...], ln_ref[...])
        ht_ref[...] = jnp.transpose(hn).astype(BF16)
        qt_ref[...] = jnp.dot(wqt_ref[...], ht_ref[...], preferred_element_type=F32).astype(BF16)
        acc_ref[...] = jnp.zeros(acc_ref.shape, F32)
        wb_ref[...] = jnp.zeros(wb_ref.shape, BF16)

        def per_head(hd, carry):
            def scores(p):
                r0 = pl.multiple_of((hd * 2 + p) * N_KEYS, N_KEYS)
                return jnp.dot(sk_ref[hd * 2 + p], qt_ref[pl.ds(r0, N_KEYS), :], preferred_element_type=F32)

            s1_ref[...] = scores(0)
            s2_ref[...] = scores(1)

            def per_lane_tile(lt, carry2):
                lanes = pl.ds(pl.multiple_of(lt * LANE, LANE), LANE)
                s1 = s1_ref[:, lanes]
                s2 = s2_ref[:, lanes]
                top1 = _largest_of_sorted(_sorted_tiles(_split_tiles(s1)), PEER_TOPK)
                top2 = _largest_of_sorted(_sorted_tiles(_split_tiles(s2)), PEER_TOPK)
                for r in range(PEER_TOPK):
                    v1_ref[r:r + 1, :] = top1[r]
                    v2_ref[r:r + 1, :] = top2[r]
                v2_lo = v2_ref[0:SUBLANES, :]
                cand = ([top1[0] + v2_lo, top1[0] + v2_ref[SUBLANES:PEER_TOPK, :]]
                        + [top1[i] + v2_lo for i in range(1, SUBLANES)]
                        + [v1_ref[SUBLANES:PEER_TOPK, :] + top2[0]])
                tau = _largest_of_sorted(_sorted_tiles(cand), PEER_TOPK)[PEER_TOPK - 1]
                top = top1[0] + top2[0]
                zsum = None
                for c in cand:
                    term = jnp.where(c >= tau, jnp.exp(c - top), 0.0)
                    zsum = term if zsum is None else zsum + term
                z = jnp.sum(zsum, axis=0, keepdims=True)
                th = tau - s1
                n_ref[hd, :, lanes] = _count_leading(v2_ref, lambda row: row >= th)
                e1_ref[hd, :, lanes] = jnp.exp(s1 - top1[0])
                r2_ref[hd, :, lanes] = _count_leading(v2_ref, lambda row: row > s2).astype(BF16)
                w2_ref[hd, :, lanes] = (jnp.exp(s2 - top2[0]) / z).astype(BF16)
                return carry2

            lax.fori_loop(0, tt // LANE, per_lane_tile, 0)
            return carry

        lax.fori_loop(0, PEER_HEADS, per_head, 0)

    n_a = ec // N_KEYS

    def step(w_cur_ref, w_prev_ref):
        zero = jnp.zeros((N_KEYS, LANE), BF16)
        a0 = pl.multiple_of(cidx * n_a, n_a)
        n_rows = [n_ref[hd, pl.ds(a0, n_a), :] for hd in range(PEER_HEADS)]
        e1_rows = [e1_ref[hd, pl.ds(a0, n_a), :] for hd in range(PEER_HEADS)]
        d = acc_ref.shape[0]
        a_per = n_a // PEER_PIECES
        d_per = d // PEER_PIECES
        for piece in range(PEER_PIECES):
            rows = slice(piece * a_per * N_KEYS, (piece + 1) * a_per * N_KEYS)
            act_p = jnp.dot(u_ref[rows, :], ht_ref[...], preferred_element_type=F32)
            for ai in range(a_per):
                aa = piece * a_per + ai
                for lt in range(tt // LANE):
                    lanes = slice(lt * LANE, (lt + 1) * LANE)
                    act = act_p[ai * N_KEYS:(ai + 1) * N_KEYS, lanes]
                    act = 0.5 * act * (1.0 + lax.erf(act * SQRT_HALF))
                    g = zero
                    for hd in range(PEER_HEADS):
                        nb = jnp.broadcast_to(n_rows[hd][aa:aa + 1, lanes], (N_KEYS, LANE)).astype(BF16)
                        eb = jnp.broadcast_to(e1_rows[hd][aa:aa + 1, lanes], (N_KEYS, LANE)).astype(BF16)
                        g = g + jnp.where(r2_ref[hd, :, lanes] < nb, w2_ref[hd, :, lanes] * eb, zero)
                    w_cur_ref[aa * N_KEYS:(aa + 1) * N_KEYS, lanes] = g * act.astype(BF16)
            drows = slice(piece * d_per, (piece + 1) * d_per)
            acc_ref[drows, :] += jnp.dot(vtp_ref[drows, :], w_prev_ref[...], preferred_element_type=F32)

    @pl.when(cidx % 2 == 0)
    def _():
        step(wa_ref, wb_ref)

    @pl.when(cidx % 2 == 1)
    def _():
        step(wb_ref, wa_ref)

    @pl.when(cidx == n_chunks - 1)
    def _():
        w_last_ref = wa_ref if (n_chunks - 1) % 2 == 0 else wb_ref
        acc = acc_ref[...] + jnp.dot(vtl_ref[...], w_last_ref[...], preferred_element_type=F32)
        y_ref[...] = x_ref[...] + jnp.transpose(acc)


def _peer(x2d, ln, wqt_b, sk_b, u_b, vt_b, tt, ec):
    n, d = x2d.shape
    n_exp = u_b.shape[0]
    assert tt % LANE == 0 and ec % (8 * N_KEYS) == 0
    n_chunks = n_exp // ec
    row = lambda i, c: (i, 0)
    const = lambda i, c: (0, 0)
    head_f32 = pltpu.VMEM((PEER_HEADS, N_KEYS, tt), F32)
    head_b16 = pltpu.VMEM((PEER_HEADS, N_KEYS, tt), BF16)
    return pl.pallas_call(
        functools.partial(_peer_kernel, ec=ec, n_chunks=n_chunks),
        grid=(n // tt, n_chunks),
        in_specs=[pl.BlockSpec((tt, d), row), pl.BlockSpec((1, d), const), pl.BlockSpec(wqt_b.shape, const),
                  pl.BlockSpec(sk_b.shape, lambda i, c: (0, 0, 0)),
                  pl.BlockSpec((ec, d), lambda i, c: (c, 0)),
                  pl.BlockSpec((d, ec), lambda i, c: (0, jnp.maximum(c - 1, 0))),
                  pl.BlockSpec((d, ec), lambda i, c: (0, n_chunks - 1))],
        out_specs=pl.BlockSpec((tt, d), row),
        out_shape=jax.ShapeDtypeStruct((n, d), F32),
        scratch_shapes=[pltpu.VMEM((d, tt), BF16), pltpu.VMEM((wqt_b.shape[0], tt), BF16),
                        head_f32, head_f32, head_b16, head_b16,
                        pltpu.VMEM((d, tt), F32), pltpu.VMEM((N_KEYS, tt), F32), pltpu.VMEM((N_KEYS, tt), F32),
                        pltpu.VMEM((PEER_TOPK, LANE), F32), pltpu.VMEM((PEER_TOPK, LANE), F32),
                        pltpu.VMEM((ec, tt), BF16), pltpu.VMEM((ec, tt), BF16)],
        compiler_params=_cparams(("parallel", "arbitrary")),
        name="peer",
    )(x2d, ln, wqt_b, sk_b, u_b, vt_b, vt_b)


def _tile(n, pref):
    t = min(n, pref)
    while n % t:
        t //= 2
    return t


def _pad_axis(a, axis, size):
    pad = [(0, 0)] * a.ndim
    pad[axis] = (0, size - a.shape[axis])
    return jnp.pad(a, pad)


def kernel(x_prompt, x_sample, cache_k, cache_v, cache_mem_k, cache_mem_v, state_hgrn, page_table, mem_prompt,
           ln_mix, w_in, qk_gain, lambda_qk, subln_gain, rel_bias, lower_bounds, hgrn_norm, w_out, ln_mem,
           mem_norm, w_mq, w_mk, w_mv, mem_qk_gain, w_mo, ln_ffn, peer_wq, peer_subkeys, peer_u, peer_v):
    B, T, D = x_prompt.shape
    DB, TS, _ = x_sample.shape
    depth = ln_mix.shape[0]
    n_pages = page_table.shape[1]
    n_mem = mem_prompt.shape[1]
    n_phys = cache_k.shape[1]
    W = HA_COLS

    lbp = jax.nn.softmax(lower_bounds.astype(F32), axis=0)
    lbs = jnp.cumsum(lbp, axis=0) - lbp[0]
    p64 = jnp.asarray(np.kron(np.eye(W // DH), np.full((DH, DH), 1.0 / DH)), BF16)
    tq = _tile(T, 256)
    tiles = _bias_tiles(rel_bias, tq)
    cache_kt = jnp.transpose(cache_k, (0, 1, 3, 4, 5, 2)).reshape(depth, n_phys, W, PAGE_SIZE)
    cache_v2 = cache_v.reshape(depth, n_phys, PAGE_SIZE * H_A, VD)
    row_head = np.arange(SROWS) // (2 * TPAD)
    tbl = rel_bias[:, row_head].T
    far_col = tbl[:, NUM_BUCKETS - 1:]
    grp_mask = jnp.asarray((np.arange(W)[None, :] // DH == np.arange(W // DH)[:, None]).astype(np.float32), BF16)

    ts_pad = 8
    yp, ys = x_prompt, x_sample
    outs = {k: [] for k in ("kp", "vp", "mkp", "mvp", "sp", "ks", "vs", "ss")}
    for l in range(depth):
        lambda_init = 0.8 - 0.6 * math.exp(-0.3 * l)
        out_scale = 1.0 - lambda_init
        lq1, lk1, lq2, lk2 = lambda_qk[l].astype(F32)
        lam = (jnp.exp(jnp.sum(lq1 * lk1)) - jnp.exp(jnp.sum(lq2 * lk2)) + lambda_init).reshape(1)
        w_in_b = w_in[l].astype(BF16)
        gq = jnp.tile(qk_gain[l, 0], W // DH).reshape(1, W)
        gk = jnp.tile(qk_gain[l, 1], W // DH).reshape(1, W)
        ln_mix_l = ln_mix[l].reshape(1, D)
        subln = subln_gain[l].reshape(1, VD)
        lb = lbs[l].reshape(1, H_B * DK)
        hgain = hgrn_norm[l].reshape(1, DV)
        woa = w_out[l, :H_A * VD].astype(BF16)
        wob = w_out[l, H_A * VD:].astype(BF16)
        ln_mem_l = ln_mem[l].reshape(1, D)
        wmq = w_mq[l].astype(BF16)
        gmq = mem_qk_gain[l, 0].reshape(1, MDH)
        gmk = mem_qk_gain[l, 1].reshape(1, MDH)
        wmo = w_mo[l].astype(BF16)
        ln_ffn_l = ln_ffn[l].reshape(1, D)
        wqt = peer_wq[l].T.astype(BF16)
        sk = peer_subkeys[l].reshape(PEER_HEADS * 2, N_KEYS, -1).astype(BF16)
        u_b = peer_u[l].astype(BF16)
        vt_b = peer_v[l].T.astype(BF16)

        mk, mv, mkb, mvb = _memkv(mem_prompt.reshape(B * n_mem, D), mem_norm[l].reshape(1, D),
                                  w_mk[l].astype(BF16), w_mv[l].astype(BF16), gmk, _tile(B * n_mem, 512))
        qab, ka, va, kab, vab, qh, fh, ih, gh = _in_proj(yp.reshape(B * T, D), ln_mix_l, w_in_b, gq, gk, p64,
                                                         _tile(B * T, 512))
        r3 = lambda a: a.reshape(B, T, W)
        oa = _prompt_attn(lam, rel_bias, r3(qab), r3(kab), r3(vab), tiles, subln, tq, out_scale)
        ob, sp = _hgrn(r3(qh), r3(fh), r3(ih), r3(gh), lb, hgain, jnp.zeros((B, H_B, DK, DV), F32),
                       _tile(T, 256), None)
        x2 = _mid(yp.reshape(B * T, D), oa.reshape(B * T, W), ob.reshape(B * T, W), woa, wob, ln_mem_l, wmq, gmq,
                  mkb.reshape(B, n_mem, W), mvb.reshape(B, n_mem, W), wmo, T, 1, _tile(T, 512))
        yp = _peer(x2, ln_ffn_l, wqt, sk, u_b, vt_b, _tile(B * T, 512), PEER_EC).reshape(B, T, D)
        outs["kp"].append(ka.reshape(B, T, H_A, 2, DH))
        outs["vp"].append(va.reshape(B, T, H_A, VD))
        outs["mkp"].append(mk.reshape(B, n_mem, MH, MDH))
        outs["mvp"].append(mv.reshape(B, n_mem, MH, MDH))
        outs["sp"].append(sp)

        qab, ka, va, kab, vab, qh, fh, ih, gh = _in_proj(ys.reshape(DB * TS, D), ln_mix_l, w_in_b, gq, gk, p64,
                                                         _tile(DB * TS, 512))
        s3 = lambda a: a.reshape(DB, TS, W)
        qs = _pad_axis(s3(qab), 1, TPAD)
        wq_s = (qs[:, None, :, :] * grp_mask[None, :, None, :]).reshape(DB, SROWS, W)
        knew_t = _pad_axis(jnp.swapaxes(s3(kab), 1, 2), 2, NEW_PAD)
        vnew = _pad_axis(s3(vab), 1, NEW_PAD)
        oa_s = _sample_attn(l, page_table, lam, wq_s, cache_kt, cache_v2, knew_t, vnew, far_col, tbl, subln,
                            _tile(n_pages, 8), TS, out_scale)
        hp = lambda a: _pad_axis(s3(a), 1, HGRN_CHUNK)
        ob_s, ss = _hgrn(hp(qh), hp(fh), hp(ih), hp(gh), lb, hgain, state_hgrn[l].astype(F32), HGRN_CHUNK, TS)
        x2 = _mid(_pad_axis(ys, 1, ts_pad).reshape(DB * ts_pad, D), oa_s.reshape(DB * ts_pad, W),
                  ob_s[:, :ts_pad].reshape(DB * ts_pad, W), woa, wob, ln_mem_l, wmq, gmq,
                  cache_mem_k[l].reshape(DB, n_mem, W).astype(BF16),
                  cache_mem_v[l].reshape(DB, n_mem, W).astype(BF16), wmo, ts_pad, DB, ts_pad)
        ys = _peer(x2, ln_ffn_l, wqt, sk, u_b, vt_b, DB * ts_pad, PEER_EC).reshape(DB, ts_pad, D)[:, :TS]
        outs["ks"].append(ka.reshape(DB, TS, H_A, 2, DH))
        outs["vs"].append(va.reshape(DB, TS, H_A, VD))
        outs["ss"].append(ss.astype(state_hgrn.dtype))

    st = lambda k: jnp.stack(outs[k])
    return (yp, ys, st("kp"), st("vp"), st("mkp"), st("mvp"), st("sp"), st("ks"), st("vs"), st("ss"))
```

```python
import functools
import math

import numpy as np
import jax
import jax.numpy as jnp
from jax import lax
from jax.experimental import pallas as pl
from jax.experimental.pallas import tpu as pltpu

F32 = jnp.float32
BF16 = jnp.bfloat16

PAGE_SIZE = 128
H_A = 4
DH = 64
VD = 2 * DH
H_B = 4
DK = 128
DV = 128
NUM_BUCKETS = 32
MAX_DISTANCE = 128
MH = 4
MDH = 128
PEER_HEADS = 8
N_KEYS = 128
PEER_TOPK = 16
EPS = 1e-6
NEG = -1e30
F_MIN = 1e-20
HA_COLS = H_A * 2 * DH
SQRT_HALF = 0.7071067811865476

VMEM_LIMIT_BYTES = 56 * 1024 * 1024
HGRN_CHUNK = 64
HGRN_LEVELS = (1, 2, 4, 8, 16, 32, 64)
PEER_EC = 1024
PEER_PIECES = 1
SAMPLE_PAGES_PER_STEP = 16
PROMPT_HEADS_PER_STEP = 2


def _cparams(sem):
    return pltpu.CompilerParams(dimension_semantics=sem, vmem_limit_bytes=VMEM_LIMIT_BYTES)


def _rms(x, g):
    return x * lax.rsqrt(jnp.mean(x * x, axis=-1, keepdims=True) + EPS) * g


def _group_mean_sq(x, p_ref):
    sq = x * x
    hi = sq.astype(BF16)
    lo = (sq - hi.astype(F32)).astype(BF16)
    p = p_ref[...]
    return jnp.dot(hi, p, preferred_element_type=F32) + jnp.dot(lo, p, preferred_element_type=F32)


def _dot_nt(a, b):
    return lax.dot_general(a, b, (((1,), (1,)), ((), ())), preferred_element_type=F32)


def _t5_bucket(rel):
    n = jnp.maximum(rel, 0)
    max_exact = NUM_BUCKETS // 2
    nf = jnp.maximum(n, max_exact).astype(F32)
    large = max_exact + (jnp.log(nf / max_exact) / math.log(MAX_DISTANCE / max_exact)
                         * (NUM_BUCKETS - max_exact)).astype(jnp.int32)
    large = jnp.minimum(large, NUM_BUCKETS - 1)
    return jnp.where(n < max_exact, n, large)


def _in_proj_kernel(x_ref, ln_ref, w_ref, gq_ref, gk_ref, p_ref,
                    qab_ref, ka_ref, va_ref, kab_ref, vab_ref, qh_ref, fh_ref, ih_ref, gh_ref):
    hb = _rms(x_ref[...], ln_ref[...]).astype(BF16)

    def proj(j):
        return jnp.dot(hb, w_ref[:, j * HA_COLS:(j + 1) * HA_COLS], preferred_element_type=F32)

    qa = proj(0)
    qn = qa * lax.rsqrt(_group_mean_sq(qa, p_ref) + EPS) * gq_ref[...]
    qab_ref[...] = (qn * (DH ** -0.5)).astype(BF16)
    ka = proj(1)
    kn = ka * lax.rsqrt(_group_mean_sq(ka, p_ref) + EPS) * gk_ref[...]
    ka_ref[...] = kn
    kab_ref[...] = kn.astype(BF16)
    va = proj(2)
    va_ref[...] = va
    vab_ref[...] = va.astype(BF16)
    qh_ref[...] = proj(3)
    fh_ref[...] = proj(4)
    ih_ref[...] = proj(5)
    gh_ref[...] = proj(6)


def _in_proj(x2d, ln, w_b, gq, gk, p64, tm):
    n, d = x2d.shape
    cols = w_b.shape[1]
    row = lambda i: (i, 0)
    const = lambda i: (0, 0)
    o_spec = pl.BlockSpec((tm, HA_COLS), row)
    f32o = jax.ShapeDtypeStruct((n, HA_COLS), F32)
    b16o = jax.ShapeDtypeStruct((n, HA_COLS), BF16)
    return pl.pallas_call(
        _in_proj_kernel,
        grid=(n // tm,),
        in_specs=[pl.BlockSpec((tm, d), row), pl.BlockSpec((1, d), const), pl.BlockSpec((d, cols), const),
                  pl.BlockSpec((1, HA_COLS), const), pl.BlockSpec((1, HA_COLS), const),
                  pl.BlockSpec((HA_COLS, HA_COLS), const)],
        out_specs=[o_spec] * 9,
        out_shape=[b16o, f32o, f32o, b16o, b16o, f32o, f32o, f32o, f32o],
        compiler_params=_cparams(("parallel",)),
        name="in_proj",
    )(x2d, ln, w_b, gq, gk, p64)


def _bias_tile_kernel(rb_ref, o_ref, *, tb):
    h = pl.program_id(0)
    d = pl.program_id(1)
    r = lax.broadcasted_iota(jnp.int32, (tb, tb), 0)
    c = lax.broadcasted_iota(jnp.int32, (tb, tb), 1)
    bucket = _t5_bucket(d * tb + r - c)
    acc = jnp.full((tb, tb), rb_ref[h], F32)
    for b in range(1, NUM_BUCKETS):
        acc = jnp.where(bucket == b, rb_ref[b * H_A + h], acc)
    o_ref[0, 0] = acc


def _bias_tiles(rel_bias, tb):
    return pl.pallas_call(
        functools.partial(_bias_tile_kernel, tb=tb),
        grid=(H_A, 2),
        in_specs=[pl.BlockSpec(memory_space=pltpu.SMEM)],
        out_specs=pl.BlockSpec((1, 1, tb, tb), lambda h, d: (h, d, 0, 0)),
        out_shape=jax.ShapeDtypeStruct((H_A, 2, tb, tb), F32),
        compiler_params=_cparams(("parallel", "parallel")),
        name="bias_tiles",
    )(rel_bias.reshape(-1))


def _prompt_attn_kernel(lam_ref, rb_ref, q_ref, k_ref, v_ref, bias_ref, g_ref, o_ref, *, tq, out_scale):
    hps = PROMPT_HEADS_PER_STEP
    hg = pl.program_id(1)
    i = pl.program_id(2)
    lane = lax.broadcasted_iota(jnp.int32, (tq, 2 * DH), 1)
    q2s, fars = [], []
    for hh in range(hps):
        q = q_ref[0, :, hh * 2 * DH:(hh + 1) * 2 * DH]
        zero = jnp.zeros_like(q)
        q2s.append(jnp.concatenate([jnp.where(lane < DH, q, zero), jnp.where(lane >= DH, q, zero)], axis=0))
        fars.append(rb_ref[(NUM_BUCKETS - 1) * H_A + hg * hps + hh])

    def update(carry, j, s_fns):
        rows = pl.ds(pl.multiple_of(j * tq, tq), tq)
        out = []
        for hh in range(hps):
            m, l, acc = carry[hh]
            kj = k_ref[0, rows, hh * 2 * DH:(hh + 1) * 2 * DH]
            vj = v_ref[0, rows, hh * VD:(hh + 1) * VD]
            s = s_fns[hh](_dot_nt(q2s[hh], kj))
            m_new = jnp.maximum(m, jnp.max(s, axis=-1, keepdims=True))
            alpha = jnp.exp(m - m_new)
            p = jnp.exp(s - m_new)
            l = alpha * l + jnp.sum(p, axis=-1, keepdims=True)
            acc = alpha * acc + jnp.dot(p.astype(BF16), vj, preferred_element_type=F32)
            out.append((m_new, l, acc))
        return tuple(out)

    init = tuple((jnp.full((2 * tq, 1), NEG, F32), jnp.zeros((2 * tq, 1), F32), jnp.zeros((2 * tq, VD), F32))
                 for _ in range(hps))
    far_fns = [lambda s, fb=fb: s + fb for fb in fars]
    carry = lax.fori_loop(0, jnp.maximum(i - 1, 0), lambda j, c: update(c, j, far_fns), init)
    near_fns = []
    for hh in range(hps):
        b1 = bias_ref[hh, 1]
        b1 = jnp.concatenate([b1, b1], axis=0)
        near_fns.append(lambda s, b1=b1: jnp.where(i >= 1, s + b1, NEG))
    carry = update(carry, jnp.maximum(i - 1, 0), near_fns)
    r = lax.broadcasted_iota(jnp.int32, (tq, tq), 0)
    c = lax.broadcasted_iota(jnp.int32, (tq, tq), 1)
    diag_fns = []
    for hh in range(hps):
        b0 = jnp.where(r >= c, bias_ref[hh, 0], NEG)
        b0 = jnp.concatenate([b0, b0], axis=0)
        diag_fns.append(lambda s, b0=b0: jnp.where(b0 > 0.5 * NEG, s + b0, NEG))
    carry = update(carry, i, diag_fns)
    for hh in range(hps):
        m, l, acc = carry[hh]
        o = acc / l
        o = o[:tq] - lam_ref[0] * o[tq:]
        o_ref[0, :, hh * VD:(hh + 1) * VD] = (_rms(o, g_ref[...]) * out_scale).astype(BF16)


def _prompt_attn(lam, rel_bias, qab, kab, vab, tiles, subln, tq, out_scale):
    b, t, _ = qab.shape
    hps = PROMPT_HEADS_PER_STEP
    blk = lambda bb, h, i: (bb, i, h)
    full = lambda bb, h, i: (bb, 0, h)
    return pl.pallas_call(
        functools.partial(_prompt_attn_kernel, tq=tq, out_scale=out_scale),
        grid=(b, H_A // hps, t // tq),
        in_specs=[pl.BlockSpec(memory_space=pltpu.SMEM), pl.BlockSpec(memory_space=pltpu.SMEM),
                  pl.BlockSpec((1, tq, hps * 2 * DH), blk), pl.BlockSpec((1, t, hps * 2 * DH), full),
                  pl.BlockSpec((1, t, hps * VD), full),
                  pl.BlockSpec((hps, 2, tq, tq), lambda bb, h, i: (h, 0, 0, 0)),
                  pl.BlockSpec((1, VD), lambda bb, h, i: (0, 0))],
        out_specs=pl.BlockSpec((1, tq, hps * VD), blk),
        out_shape=jax.ShapeDtypeStruct((b, t, H_A * VD), BF16),
        compiler_params=_cparams(("parallel", "parallel", "arbitrary")),
        name="prompt_attn",
    )(lam, rel_bias.reshape(-1), qab, kab, vab, tiles, subln)


TPAD = 8
SROWS = H_A * 2 * TPAD
NEW_PAD = 128


def _sample_attn_kernel(pt_ref, lam_ref, wq_ref, *refs, pp, n_pages, ts, out_scale):
    k_refs = refs[:pp]
    v_refs = refs[pp:2 * pp]
    (knew_ref, vnew_ref, far_ref, tbl_ref, g_ref, o_ref, m_ref, l_ref, acc_ref) = refs[2 * pp:]
    j = pl.program_id(1)
    nsteps = n_pages // pp
    past = n_pages * PAGE_SIZE

    @pl.when(j == 0)
    def _():
        m_ref[...] = jnp.full(m_ref.shape, NEG, F32)
        l_ref[...] = jnp.zeros(l_ref.shape, F32)
        acc_ref[...] = jnp.zeros(acc_ref.shape, F32)

    wq = wq_ref[0]

    def update(s, v_heads):
        m = m_ref[...]
        m_new = jnp.maximum(m, jnp.max(s, axis=1, keepdims=True))
        alpha = jnp.exp(m - m_new)
        p = jnp.exp(s - m_new)
        l_ref[...] = alpha * l_ref[...] + jnp.sum(p, axis=1, keepdims=True)
        m_ref[...] = m_new
        pb = p.astype(BF16)
        for h in range(H_A):
            rows = slice(h * 2 * TPAD, (h + 1) * 2 * TPAD)
            acc_ref[rows, :] = acc_ref[rows, :] * alpha[rows] + jnp.dot(pb[rows], v_heads[h],
                                                                        preferred_element_type=F32)

    def near_bias(rel):
        bucket = _t5_bucket(rel)
        acc = jnp.broadcast_to(tbl_ref[:, 0:1], rel.shape)
        for b in range(1, NUM_BUCKETS):
            acc = jnp.where(bucket == b, tbl_ref[:, b:b + 1], acc)
        return acc

    def tok_of_row(shape):
        return lax.broadcasted_iota(jnp.int32, shape, 0) % TPAD

    far = far_ref[...]
    s_pages = [jnp.dot(wq, k_refs[u][0, 0].astype(BF16), preferred_element_type=F32) for u in range(pp)]
    v_heads = [jnp.concatenate([v_refs[u][0, 0, pl.ds(h, PAGE_SIZE, stride=H_A), :].astype(BF16)
                                for u in range(pp)], axis=0) for h in range(H_A)]

    @pl.when(j < nsteps - 1)
    def _():
        update(jnp.concatenate(s_pages, axis=1) + far, v_heads)

    @pl.when(j == nsteps - 1)
    def _():
        shape = (SROWS, PAGE_SIZE)
        kpos = (n_pages - 1) * PAGE_SIZE + lax.broadcasted_iota(jnp.int32, shape, 1)
        rel = past + tok_of_row(shape) - kpos
        s_last = s_pages[pp - 1] + near_bias(rel)
        update(jnp.concatenate([sp + far for sp in s_pages[:pp - 1]] + [s_last], axis=1), v_heads)
        shape = (SROWS, NEW_PAD)
        kt = lax.broadcasted_iota(jnp.int32, shape, 1)
        rel = tok_of_row(shape) - kt
        s_new = jnp.dot(wq, knew_ref[0], preferred_element_type=F32) + near_bias(rel)
        s_new = jnp.where((rel >= 0) & (kt < ts), s_new, NEG)
        update(s_new, [vnew_ref[0, :, h * VD:(h + 1) * VD] for h in range(H_A)])
        o = acc_ref[...] / l_ref[...]
        lam = lam_ref[0]
        for h in range(H_A):
            r0 = h * 2 * TPAD
            oh = o[r0:r0 + TPAD] - lam * o[r0 + TPAD:r0 + 2 * TPAD]
            o_ref[0, :, h * VD:(h + 1) * VD] = (_rms(oh, g_ref[...]) * out_scale).astype(o_ref.dtype)


def _sample_attn(layer, page_table, lam, wq, cache_kt, cache_v2, knew_t, vnew, far_col, tbl, subln, pp, ts,
                 out_scale):
    db, n_pages = page_table.shape
    nsteps = n_pages // pp

    def page_map(u):
        return lambda b, j, pt: (layer, pt[b * n_pages + j * pp + u], 0, 0)

    per_b = lambda b, j, pt: (b, 0, 0)
    const = lambda b, j, pt: (0, 0)
    page_spec = [pl.BlockSpec((1, 1, HA_COLS, PAGE_SIZE), page_map(u)) for u in range(pp)]
    grid_spec = pltpu.PrefetchScalarGridSpec(
        num_scalar_prefetch=1,
        grid=(db, nsteps),
        in_specs=[pl.BlockSpec(memory_space=pltpu.SMEM), pl.BlockSpec((1, SROWS, HA_COLS), per_b)]
        + page_spec + page_spec
        + [pl.BlockSpec((1, HA_COLS, NEW_PAD), per_b), pl.BlockSpec((1, NEW_PAD, H_A * VD), per_b),
           pl.BlockSpec((SROWS, 1), const), pl.BlockSpec((SROWS, NUM_BUCKETS), const),
           pl.BlockSpec((1, VD), const)],
        out_specs=pl.BlockSpec((1, TPAD, H_A * VD), per_b),
        scratch_shapes=[pltpu.VMEM((SROWS, 1), F32), pltpu.VMEM((SROWS, 1), F32), pltpu.VMEM((SROWS, VD), F32)],
    )
    return pl.pallas_call(
        functools.partial(_sample_attn_kernel, pp=pp, n_pages=n_pages, ts=ts, out_scale=out_scale),
        grid_spec=grid_spec,
        out_shape=jax.ShapeDtypeStruct((db, TPAD, H_A * VD), BF16),
        compiler_params=_cparams(("parallel", "arbitrary")),
        name="sample_attn",
    )(page_table.reshape(-1), lam, wq, *([cache_kt] * pp), *([cache_v2] * pp), knew_t, vnew, far_col, tbl, subln)


def _hgrn_consts(chunk):
    t = np.arange(chunk)
    rows = []
    for m in HGRN_LEVELS:
        same = (t[:, None] // m) == (t[None, :] // m)
        rows.append(same & (t[None, :] <= t[:, None]))
        rows.append(same & (t[None, :] > t[:, None]))
    cum = np.concatenate(rows, axis=0).astype(np.float32)
    masks = []
    for m in HGRN_LEVELS[:-1]:
        bt, bs = t[:, None] // m, t[None, :] // m
        masks.append((bt == bs + 1) & (bs % 2 == 0))
    masks.append(t[:, None] == t[None, :])
    return jnp.asarray(cum, BF16), jnp.asarray(np.stack(masks).astype(np.float32))


def _hgrn_kernel(q_ref, f_ref, i_ref, g_ref, lb_ref, gain_ref, s0_ref, cum_ref, mask_ref,
                 o_ref, s_out_ref, st_ref, *, tblk, valid_len):
    c = HGRN_CHUNK
    nlev = len(HGRN_LEVELS)
    tstep = pl.program_id(1)

    @pl.when(tstep == 0)
    def _():
        for h in range(H_B):
            st_ref[h] = jnp.transpose(s0_ref[0, h])

    lb = lb_ref[...]
    cum = cum_ref[...]
    for ci in range(tblk // c):
        rows = slice(ci * c, (ci + 1) * c)
        f = lb + (1.0 - lb) * jax.nn.sigmoid(f_ref[0, rows, :])
        g = jnp.log(jnp.maximum(f, F_MIN))
        kh = 1.0 - f
        if valid_len is not None:
            pos = tstep * tblk + ci * c + lax.broadcasted_iota(jnp.int32, g.shape, 0)
            g = jnp.where(pos < valid_len, g, 0.0)
            kh = jnp.where(pos < valid_len, kh, 0.0)
        g_hi = g.astype(BF16)
        r1 = g - g_hi.astype(F32)
        g_mid = r1.astype(BF16)
        g_lo = (r1 - g_mid.astype(F32)).astype(BF16)
        ce = (jnp.dot(cum, g_hi, preferred_element_type=F32) + jnp.dot(cum, g_mid, preferred_element_type=F32)
              + jnp.dot(cum, g_lo, preferred_element_type=F32))
        for h in range(H_B):
            lanes = slice(h * DK, (h + 1) * DK)
            q = q_ref[0, rows, lanes]
            k = kh[:, lanes]
            vb = i_ref[0, rows, lanes].astype(BF16)
            a = jnp.where(mask_ref[nlev - 1] > 0.5, _dot_nt(q.astype(BF16), k.astype(BF16)), 0.0)
            for lv in range(nlev - 1):
                cm = ce[(2 * lv) * c:(2 * lv + 1) * c, lanes]
                em = ce[(2 * lv + 1) * c:(2 * lv + 2) * c, lanes]
                qt = (q * jnp.exp(cm)).astype(BF16)
                kt = (k * jnp.exp(em)).astype(BF16)
                a = a + jnp.where(mask_ref[lv] > 0.5, _dot_nt(qt, kt), 0.0)
            b_incl = ce[(2 * nlev - 2) * c:(2 * nlev - 1) * c, lanes]
            b_rest = ce[(2 * nlev - 1) * c:(2 * nlev) * c, lanes]
            st = st_ref[h]
            o = _dot_nt((q * jnp.exp(b_incl)).astype(BF16), st.astype(BF16))
            o = o + jnp.dot(a.astype(BF16), vb, preferred_element_type=F32)
            k_end = (k * jnp.exp(b_rest)).astype(BF16)
            vt = jnp.transpose(i_ref[0, rows, lanes]).astype(BF16)
            decay_end = jnp.exp(b_incl[c - 1:c, :])
            st_ref[h] = st * decay_end + jnp.dot(vt, k_end, preferred_element_type=F32)
            gate = g_ref[0, rows, lanes]
            o = _rms(o, gain_ref[...]) * (gate * jax.nn.sigmoid(gate))
            o_ref[0, rows, lanes] = o.astype(BF16)

    @pl.when(tstep == pl.num_programs(1) - 1)
    def _():
        for h in range(H_B):
            s_out_ref[0, h] = jnp.transpose(st_ref[h])


def _hgrn(qh, fh, ih, gh, lb, gain, s0, tblk, valid_len):
    b, t, w = qh.shape
    cum, masks = _hgrn_consts(HGRN_CHUNK)
    blk = lambda bb, i: (bb, i, 0)
    const2 = lambda bb, i: (0, 0)
    spec = pl.BlockSpec((1, tblk, w), blk)
    s_spec = pl.BlockSpec((1, H_B, DK, DV), lambda bb, i: (bb, 0, 0, 0))
    return pl.pallas_call(
        functools.partial(_hgrn_kernel, tblk=tblk, valid_len=valid_len),
        grid=(b, t // tblk),
        in_specs=[spec, spec, spec, spec, pl.BlockSpec((1, w), const2), pl.BlockSpec((1, DV), const2), s_spec,
                  pl.BlockSpec(cum.shape, const2), pl.BlockSpec(masks.shape, lambda bb, i: (0, 0, 0))],
        out_specs=[spec, s_spec],
        out_shape=[jax.ShapeDtypeStruct((b, t, w), BF16), jax.ShapeDtypeStruct((b, H_B, DK, DV), F32)],
        scratch_shapes=[pltpu.VMEM((H_B, DV, DK), F32)],
        compiler_params=_cparams(("parallel", "arbitrary")),
        name="hgrn",
    )(qh, fh, ih, gh, lb, gain, s0, cum, masks)


def _memkv_kernel(x_ref, ln_ref, wk_ref, wv_ref, gk_ref, k_ref, v_ref, kb_ref, vb_ref):
    mb = _rms(x_ref[...], ln_ref[...]).astype(BF16)
    k = jnp.dot(mb, wk_ref[...], preferred_element_type=F32)
    for h in range(MH):
        lanes = slice(h * MDH, (h + 1) * MDH)
        kn = _rms(k[:, lanes], gk_ref[...])
        k_ref[:, lanes] = kn
        kb_ref[:, lanes] = kn.astype(BF16)
    v = jnp.dot(mb, wv_ref[...], preferred_element_type=F32)
    v_ref[...] = v
    vb_ref[...] = v.astype(BF16)


def _memkv(mem2d, ln, wk_b, wv_b, gk, tm):
    n, d = mem2d.shape
    w = MH * MDH
    row = lambda i: (i, 0)
    const = lambda i: (0, 0)
    o_spec = pl.BlockSpec((tm, w), row)
    return pl.pallas_call(
        _memkv_kernel,
        grid=(n // tm,),
        in_specs=[pl.BlockSpec((tm, d), row), pl.BlockSpec((1, d), const), pl.BlockSpec((d, w), const),
                  pl.BlockSpec((d, w), const), pl.BlockSpec((1, MDH), const)],
        out_specs=[o_spec] * 4,
        out_shape=[jax.ShapeDtypeStruct((n, w), F32), jax.ShapeDtypeStruct((n, w), F32),
                   jax.ShapeDtypeStruct((n, w), BF16), jax.ShapeDtypeStruct((n, w), BF16)],
        compiler_params=_cparams(("parallel",)),
        name="memkv",
    )(mem2d, ln, wk_b, wv_b, gk)


def _mid_kernel(x_ref, oa_ref, ob_ref, woa_ref, wob_ref, ln_ref, wq_ref, gq_ref, mk_ref, mv_ref, wo_ref,
                y_ref, qm_ref, om_ref, *, bb, tt):
    x = x_ref[...]
    x = x + jnp.dot(oa_ref[...], woa_ref[...], preferred_element_type=F32) + jnp.dot(
        ob_ref[...], wob_ref[...], preferred_element_type=F32)
    hb = _rms(x, ln_ref[...]).astype(BF16)
    qm = jnp.dot(hb, wq_ref[...], preferred_element_type=F32)
    for h in range(MH):
        lanes = slice(h * MDH, (h + 1) * MDH)
        qm_ref[:, lanes] = _rms(qm[:, lanes], gq_ref[...]) * (MDH ** -0.5)

    def per_batch(bi, carry):
        r = pl.ds(pl.multiple_of(bi * tt, tt), tt)
        for h in range(MH):
            lanes = slice(h * MDH, (h + 1) * MDH)
            s = _dot_nt(qm_ref[r, lanes].astype(BF16), mk_ref[bi, :, lanes])
            s = s - jnp.max(s, axis=-1, keepdims=True)
            p = jnp.exp(s)
            p = p / jnp.sum(p, axis=-1, keepdims=True)
            om_ref[r, lanes] = jnp.dot(p.astype(BF16), mv_ref[bi, :, lanes], preferred_element_type=F32)
        return carry

    lax.fori_loop(0, bb, per_batch, 0)
    y_ref[...] = x + jnp.dot(om_ref[...].astype(BF16), wo_ref[...], preferred_element_type=F32)


def _mid(x2d, oa, ob, woa, wob, ln, wq, gq, mkb, mvb, wo, t, bb, tt):
    n, d = x2d.shape
    n_mem = mkb.shape[1]
    w = H_A * VD
    assert bb == 1 or tt == t
    per_batch = t // tt
    row = lambda i: (i, 0)
    const = lambda i: (0, 0)
    mem_spec = pl.BlockSpec((bb, n_mem, MH * MDH), lambda i: (i // per_batch, 0, 0))
    rows = bb * tt
    return pl.pallas_call(
        functools.partial(_mid_kernel, bb=bb, tt=tt),
        grid=(n // rows,),
        in_specs=[pl.BlockSpec((rows, d), row), pl.BlockSpec((rows, w), row), pl.BlockSpec((rows, w), row),
                  pl.BlockSpec((w, d), const), pl.BlockSpec((w, d), const), pl.BlockSpec((1, d), const),
                  pl.BlockSpec((d, MH * MDH), const), pl.BlockSpec((1, MDH), const), mem_spec, mem_spec,
                  pl.BlockSpec((MH * MDH, d), const)],
        out_specs=pl.BlockSpec((rows, d), row),
        out_shape=jax.ShapeDtypeStruct((n, d), F32),
        scratch_shapes=[pltpu.VMEM((rows, MH * MDH), F32), pltpu.VMEM((rows, MH * MDH), F32)],
        compiler_params=_cparams(("parallel",)),
        name="mid",
    )(x2d, oa, ob, woa, wob, ln, wq, gq, mkb, mvb, wo)


LANE = 128


SUBLANES = 8


def _sort_network(n):
    pairs = []
    p = 1
    while p < n:
        k = p
        while k >= 1:
            for j in range(k % p, n - k, 2 * k):
                for i in range(min(k, n - j - k)):
                    if (i + j) // (2 * p) == (i + j + k) // (2 * p):
                        pairs.append((i + j, i + j + k))
            k //= 2
        p *= 2
    return pairs


def _sorted_tiles(tiles):
    wires = list(tiles) + [None] * (PEER_TOPK - len(tiles))
    for a, b in _sort_network(PEER_TOPK):
        if wires[a] is None:
            wires[a], wires[b] = wires[b], None
        elif wires[b] is not None:
            wires[a], wires[b] = jnp.maximum(wires[a], wires[b]), jnp.minimum(wires[a], wires[b])
    return [w for w in wires if w is not None]


def _largest_of_sorted(vs, count):
    out = []
    for r in range(count):
        m = jnp.max(vs[0], axis=0, keepdims=True)
        out.append(m)
        need = count - r - 1
        if need == 0:
            break
        eq = vs[0] == m
        vs = [jnp.where(eq, vs[i + 1] if i + 1 < len(vs) else -jnp.inf, vs[i]) for i in range(min(len(vs), need))]
    return out


def _split_tiles(x):
    return [x[i * SUBLANES:(i + 1) * SUBLANES] for i in range(x.shape[0] // SUBLANES)]


def _count_leading(rows_ref, pred):
    cnt = None
    for r in range(PEER_TOPK):
        hit = pred(rows_ref[r:r + 1, :])
        cnt = jnp.where(hit, float(r + 1), 0.0 if cnt is None else cnt)
    return cnt


def _peer_kernel(x_ref, ln_ref, wqt_ref, sk_ref, u_ref, vtp_ref, vtl_ref, y_ref,
                 ht_ref, qt_ref, n_ref, e1_ref, r2_ref, w2_ref, acc_ref, s1_ref, s2_ref, v1_ref, v2_ref,
                 wa_ref, wb_ref, *, ec, n_chunks):
    cidx = pl.program_id(1)
    tt = x_ref.shape[0]

    @pl.when(cidx == 0)
    def _():
        hn = _rms(x_ref[...], ln_ref[...])
        ht_ref[...] = jnp.transpose(hn).astype(BF16)
        qt_ref[...] = jnp.dot(wqt_ref[...], ht_ref[...], preferred_element_type=F32).astype(BF16)
        acc_ref[...] = jnp.zeros(acc_ref.shape, F32)
        wb_ref[...] = jnp.zeros(wb_ref.shape, BF16)

        def per_head(hd, carry):
            def scores(p):
                r0 = pl.multiple_of((hd * 2 + p) * N_KEYS, N_KEYS)
                return jnp.dot(sk_ref[hd * 2 + p], qt_ref[pl.ds(r0, N_KEYS), :], preferred_element_type=F32)

            s1_ref[...] = scores(0)
            s2_ref[...] = scores(1)

            def per_lane_tile(lt, carry2):
                lanes = pl.ds(pl.multiple_of(lt * LANE, LANE), LANE)
                s1 = s1_ref[:, lanes]
                s2 = s2_ref[:, lanes]
                top1 = _largest_of_sorted(_sorted_tiles(_split_tiles(s1)), PEER_TOPK)
                top2 = _largest_of_sorted(_sorted_tiles(_split_tiles(s2)), PEER_TOPK)
                for r in range(PEER_TOPK):
                    v1_ref[r:r + 1, :] = top1[r]
                    v2_ref[r:r + 1, :] = top2[r]
                v2_lo = v2_ref[0:SUBLANES, :]
                cand = ([top1[0] + v2_lo, top1[0] + v2_ref[SUBLANES:PEER_TOPK, :]]
                        + [top1[i] + v2_lo for i in range(1, SUBLANES)]
                        + [v1_ref[SUBLANES:PEER_TOPK, :] + top2[0]])
                tau = _largest_of_sorted(_sorted_tiles(cand), PEER_TOPK)[PEER_TOPK - 1]
                top = top1[0] + top2[0]
                zsum = None
                for c in cand:
                    term = jnp.where(c >= tau, jnp.exp(c - top), 0.0)
                    zsum = term if zsum is None else zsum + term
                z = jnp.sum(zsum, axis=0, keepdims=True)
                th = tau - s1
                n_ref[hd, :, lanes] = _count_leading(v2_ref, lambda row: row >= th)
                e1_ref[hd, :, lanes] = jnp.exp(s1 - top1[0])
                r2_ref[hd, :, lanes] = _count_leading(v2_ref, lambda row: row > s2).astype(BF16)
                w2_ref[hd, :, lanes] = (jnp.exp(s2 - top2[0]) / z).astype(BF16)
                return carry2

            lax.fori_loop(0, tt // LANE, per_lane_tile, 0)
            return carry

        lax.fori_loop(0, PEER_HEADS, per_head, 0)

    n_a = ec // N_KEYS

    def step(w_cur_ref, w_prev_ref):
        zero = jnp.zeros((N_KEYS, LANE), BF16)
        a0 = pl.multiple_of(cidx * n_a, n_a)
        n_rows = [n_ref[hd, pl.ds(a0, n_a), :] for hd in range(PEER_HEADS)]
        e1_rows = [e1_ref[hd, pl.ds(a0, n_a), :] for hd in range(PEER_HEADS)]
        d = acc_ref.shape[0]
        a_per = n_a // PEER_PIECES
        d_per = d // PEER_PIECES
        for piece in range(PEER_PIECES):
            rows = slice(piece * a_per * N_KEYS, (piece + 1) * a_per * N_KEYS)
            act_p = jnp.dot(u_ref[rows, :], ht_ref[...], preferred_element_type=F32)
            for ai in range(a_per):
                aa = piece * a_per + ai
                for lt in range(tt // LANE):
                    lanes = slice(lt * LANE, (lt + 1) * LANE)
                    act = act_p[ai * N_KEYS:(ai + 1) * N_KEYS, lanes]
                    act = 0.5 * act * (1.0 + lax.erf(act * SQRT_HALF))
                    g = zero
                    for hd in range(PEER_HEADS):
                        nb = jnp.broadcast_to(n_rows[hd][aa:aa + 1, lanes], (N_KEYS, LANE)).astype(BF16)
                        eb = jnp.broadcast_to(e1_rows[hd][aa:aa + 1, lanes], (N_KEYS, LANE)).astype(BF16)
                        g = g + jnp.where(r2_ref[hd, :, lanes] < nb, w2_ref[hd, :, lanes] * eb, zero)
                    w_cur_ref[aa * N_KEYS:(aa + 1) * N_KEYS, lanes] = g * act.astype(BF16)
            drows = slice(piece * d_per, (piece + 1) * d_per)
            acc_ref[drows, :] += jnp.dot(vtp_ref[drows, :], w_prev_ref[...], preferred_element_type=F32)

    @pl.when(cidx % 2 == 0)
    def _():
        step(wa_ref, wb_ref)

    @pl.when(cidx % 2 == 1)
    def _():
        step(wb_ref, wa_ref)

    @pl.when(cidx == n_chunks - 1)
    def _():
        w_last_ref = wa_ref if (n_chunks - 1) % 2 == 0 else wb_ref
        acc = acc_ref[...] + jnp.dot(vtl_ref[...], w_last_ref[...], preferred_element_type=F32)
        y_ref[...] = x_ref[...] + jnp.transpose(acc)


def _peer(x2d, ln, wqt_b, sk_b, u_b, vt_b, tt, ec):
    n, d = x2d.shape
    n_exp = u_b.shape[0]
    assert tt % LANE == 0 and ec % (8 * N_KEYS) == 0
    n_chunks = n_exp // ec
    row = lambda i, c: (i, 0)
    const = lambda i, c: (0, 0)
    head_f32 = pltpu.VMEM((PEER_HEADS, N_KEYS, tt), F32)
    head_b16 = pltpu.VMEM((PEER_HEADS, N_KEYS, tt), BF16)
    return pl.pallas_call(
        functools.partial(_peer_kernel, ec=ec, n_chunks=n_chunks),
        grid=(n // tt, n_chunks),
        in_specs=[pl.BlockSpec((tt, d), row), pl.BlockSpec((1, d), const), pl.BlockSpec(wqt_b.shape, const),
                  pl.BlockSpec(sk_b.shape, lambda i, c: (0, 0, 0)),
                  pl.BlockSpec((ec, d), lambda i, c: (c, 0)),
                  pl.BlockSpec((d, ec), lambda i, c: (0, jnp.maximum(c - 1, 0))),
                  pl.BlockSpec((d, ec), lambda i, c: (0, n_chunks - 1))],
        out_specs=pl.BlockSpec((tt, d), row),
        out_shape=jax.ShapeDtypeStruct((n, d), F32),
        scratch_shapes=[pltpu.VMEM((d, tt), BF16), pltpu.VMEM((wqt_b.shape[0], tt), BF16),
                        head_f32, head_f32, head_b16, head_b16,
                        pltpu.VMEM((d, tt), F32), pltpu.VMEM((N_KEYS, tt), F32), pltpu.VMEM((N_KEYS, tt), F32),
                        pltpu.VMEM((PEER_TOPK, LANE), F32), pltpu.VMEM((PEER_TOPK, LANE), F32),
                        pltpu.VMEM((ec, tt), BF16), pltpu.VMEM((ec, tt), BF16)],
        compiler_params=_cparams(("parallel", "arbitrary")),
        name="peer",
    )(x2d, ln, wqt_b, sk_b, u_b, vt_b, vt_b)


def _tile(n, pref):
    t = min(n, pref)
    while n % t:
        t //= 2
    return t


def _pad_axis(a, axis, size):
    pad = [(0, 0)] * a.ndim
    pad[axis] = (0, size - a.shape[axis])
    return jnp.pad(a, pad)


def kernel(x_prompt, x_sample, cache_k, cache_v, cache_mem_k, cache_mem_v, state_hgrn, page_table, mem_prompt,
           ln_mix, w_in, qk_gain, lambda_qk, subln_gain, rel_bias, lower_bounds, hgrn_norm, w_out, ln_mem,
           mem_norm, w_mq, w_mk, w_mv, mem_qk_gain, w_mo, ln_ffn, peer_wq, peer_subkeys, peer_u, peer_v):
    B, T, D = x_prompt.shape
    DB, TS, _ = x_sample.shape
    depth = ln_mix.shape[0]
    n_pages = page_table.shape[1]
    n_mem = mem_prompt.shape[1]
    n_phys = cache_k.shape[1]
    W = HA_COLS

    lbp = jax.nn.softmax(lower_bounds.astype(F32), axis=0)
    lbs = jnp.cumsum(lbp, axis=0) - lbp[0]
    p64 = jnp.asarray(np.kron(np.eye(W // DH), np.full((DH, DH), 1.0 / DH)), BF16)
    tq = _tile(T, 256)
    tiles = _bias_tiles(rel_bias, tq)
    cache_kt = jnp.transpose(cache_k, (0, 1, 3, 4, 5, 2)).reshape(depth, n_phys, W, PAGE_SIZE)
    cache_v2 = cache_v.reshape(depth, n_phys, PAGE_SIZE * H_A, VD)
    row_head = np.arange(SROWS) // (2 * TPAD)
    tbl = rel_bias[:, row_head].T
    far_col = tbl[:, NUM_BUCKETS - 1:]
    grp_mask = jnp.asarray((np.arange(W)[None, :] // DH == np.arange(W // DH)[:, None]).astype(np.float32), BF16)

    ts_pad = 8
    yp, ys = x_prompt, x_sample
    outs = {k: [] for k in ("kp", "vp", "mkp", "mvp", "sp", "ks", "vs", "ss")}
    for l in range(depth):
        lambda_init = 0.8 - 0.6 * math.exp(-0.3 * l)
        out_scale = 1.0 - lambda_init
        lq1, lk1, lq2, lk2 = lambda_qk[l].astype(F32)
        lam = (jnp.exp(jnp.sum(lq1 * lk1)) - jnp.exp(jnp.sum(lq2 * lk2)) + lambda_init).reshape(1)
        w_in_b = w_in[l].astype(BF16)
        gq = jnp.tile(qk_gain[l, 0], W // DH).reshape(1, W)
        gk = jnp.tile(qk_gain[l, 1], W // DH).reshape(1, W)
        ln_mix_l = ln_mix[l].reshape(1, D)
        subln = subln_gain[l].reshape(1, VD)
        lb = lbs[l].reshape(1, H_B * DK)
        hgain = hgrn_norm[l].reshape(1, DV)
        woa = w_out[l, :H_A * VD].astype(BF16)
        wob = w_out[l, H_A * VD:].astype(BF16)
        ln_mem_l = ln_mem[l].reshape(1, D)
        wmq = w_mq[l].astype(BF16)
        gmq = mem_qk_gain[l, 0].reshape(1, MDH)
        gmk = mem_qk_gain[l, 1].reshape(1, MDH)
        wmo = w_mo[l].astype(BF16)
        ln_ffn_l = ln_ffn[l].reshape(1, D)
        wqt = peer_wq[l].T.astype(BF16)
        sk = peer_subkeys[l].reshape(PEER_HEADS * 2, N_KEYS, -1).astype(BF16)
        u_b = peer_u[l].astype(BF16)
        vt_b = peer_v[l].T.astype(BF16)

        mk, mv, mkb, mvb = _memkv(mem_prompt.reshape(B * n_mem, D), mem_norm[l].reshape(1, D),
                                  w_mk[l].astype(BF16), w_mv[l].astype(BF16), gmk, _tile(B * n_mem, 512))
        qab, ka, va, kab, vab, qh, fh, ih, gh = _in_proj(yp.reshape(B * T, D), ln_mix_l, w_in_b, gq, gk, p64,
                                                         _tile(B * T, 512))
        r3 = lambda a: a.reshape(B, T, W)
        oa = _prompt_attn(lam, rel_bias, r3(qab), r3(kab), r3(vab), tiles, subln, tq, out_scale)
        ob, sp = _hgrn(r3(qh), r3(fh), r3(ih), r3(gh), lb, hgain, jnp.zeros((B, H_B, DK, DV), F32),
                       _tile(T, 256), None)
        x2 = _mid(yp.reshape(B * T, D), oa.reshape(B * T, W), ob.reshape(B * T, W), woa, wob, ln_mem_l, wmq, gmq,
                  mkb.reshape(B, n_mem, W), mvb.reshape(B, n_mem, W), wmo, T, 1, _tile(T, 512))
        yp = _peer(x2, ln_ffn_l, wqt, sk, u_b, vt_b, _tile(B * T, 512), PEER_EC).reshape(B, T, D)
        outs["kp"].append(ka.reshape(B, T, H_A, 2, DH))
        outs["vp"].append(va.reshape(B, T, H_A, VD))
        outs["mkp"].append(mk.reshape(B, n_mem, MH, MDH))
        outs["mvp"].append(mv.reshape(B, n_mem, MH, MDH))
        outs["sp"].append(sp)

        qab, ka, va, kab, vab, qh, fh, ih, gh = _in_proj(ys.reshape(DB * TS, D), ln_mix_l, w_in_b, gq, gk, p64,
                                                         _tile(DB * TS, 512))
        s3 = lambda a: a.reshape(DB, TS, W)
        qs = _pad_axis(s3(qab), 1, TPAD)
        wq_s = (qs[:, None, :, :] * grp_mask[None, :, None, :]).reshape(DB, SROWS, W)
        knew_t = _pad_axis(jnp.swapaxes(s3(kab), 1, 2), 2, NEW_PAD)
        vnew = _pad_axis(s3(vab), 1, NEW_PAD)
        oa_s = _sample_attn(l, page_table, lam, wq_s, cache_kt, cache_v2, knew_t, vnew, far_col, tbl, subln,
                            _tile(n_pages, SAMPLE_PAGES_PER_STEP), TS, out_scale)
        hp = lambda a: _pad_axis(s3(a), 1, HGRN_CHUNK)
        ob_s, ss = _hgrn(hp(qh), hp(fh), hp(ih), hp(gh), lb, hgain, state_hgrn[l].astype(F32), HGRN_CHUNK, TS)
        x2 = _mid(_pad_axis(ys, 1, ts_pad).reshape(DB * ts_pad, D), oa_s.reshape(DB * ts_pad, W),
                  ob_s[:, :ts_pad].reshape(DB * ts_pad, W), woa, wob, ln_mem_l, wmq, gmq,
                  cache_mem_k[l].reshape(DB, n_mem, W).astype(BF16),
                  cache_mem_v[l].reshape(DB, n_mem, W).astype(BF16), wmo, ts_pad, DB, ts_pad)
        ys = _peer(x2, ln_ffn_l, wqt, sk, u_b, vt_b, DB * ts_pad, PEER_EC).reshape(DB, ts_pad, D)[:, :TS]
        outs["ks"].append(ka.reshape(DB, TS, H_A, 2, DH))
        outs["vs"].append(va.reshape(DB, TS, H_A, VD))
        outs["ss"].append(ss.astype(state_hgrn.dtype))

    st = lambda k: jnp.stack(outs[k])
    return (yp, ys, st("kp"), st("vp"), st("mkp"), st("mvp"), st("sp"), st("ks"), st("vs"), st("ss"))
```

```python
import functools
import math

import numpy as np
import jax
import jax.numpy as jnp
from jax import lax
from jax.experimental import pallas as pl
from jax.experimental.pallas import tpu as pltpu

F32 = jnp.float32
BF16 = jnp.bfloat16

PAGE_SIZE = 128
H_A = 4
DH = 64
VD = 2 * DH
H_B = 4
DK = 128
DV = 128
NUM_BUCKETS = 32
MAX_DISTANCE = 128
MH = 4
MDH = 128
PEER_HEADS = 8
N_KEYS = 128
PEER_TOPK = 16
EPS = 1e-6
NEG = -1e30
F_MIN = 1e-20
HA_COLS = H_A * 2 * DH
SQRT_HALF = 0.7071067811865476

VMEM_LIMIT_BYTES = 56 * 1024 * 1024
HGRN_CHUNK = 64
HGRN_LEVELS = (1, 2, 4, 8, 16, 32, 64)
PEER_EC = 1024
PEER_DMA_SPLIT = 4
SAMPLE_PAGES_PER_STEP = 16
PROMPT_HEADS_PER_STEP = 2


def _cparams(sem):
    return pltpu.CompilerParams(dimension_semantics=sem, vmem_limit_bytes=VMEM_LIMIT_BYTES)


def _rms(x, g):
    return x * lax.rsqrt(jnp.mean(x * x, axis=-1, keepdims=True) + EPS) * g


def _group_mean_sq(x, p_ref):
    sq = x * x
    hi = sq.astype(BF16)
    lo = (sq - hi.astype(F32)).astype(BF16)
    p = p_ref[...]
    return jnp.dot(hi, p, preferred_element_type=F32) + jnp.dot(lo, p, preferred_element_type=F32)


def _dot_nt(a, b):
    return lax.dot_general(a, b, (((1,), (1,)), ((), ())), preferred_element_type=F32)


def _t5_bucket(rel):
    n = jnp.maximum(rel, 0)
    max_exact = NUM_BUCKETS // 2
    nf = jnp.maximum(n, max_exact).astype(F32)
    large = max_exact + (jnp.log(nf / max_exact) / math.log(MAX_DISTANCE / max_exact)
                         * (NUM_BUCKETS - max_exact)).astype(jnp.int32)
    large = jnp.minimum(large, NUM_BUCKETS - 1)
    return jnp.where(n < max_exact, n, large)


def _in_proj_kernel(x_ref, ln_ref, w_ref, gq_ref, gk_ref, p_ref,
                    qab_ref, ka_ref, va_ref, kab_ref, vab_ref, qh_ref, fh_ref, ih_ref, gh_ref):
    hb = _rms(x_ref[...], ln_ref[...]).astype(BF16)

    def proj(j):
        return jnp.dot(hb, w_ref[:, j * HA_COLS:(j + 1) * HA_COLS], preferred_element_type=F32)

    qa = proj(0)
    qn = qa * lax.rsqrt(_group_mean_sq(qa, p_ref) + EPS) * gq_ref[...]
    qab_ref[...] = (qn * (DH ** -0.5)).astype(BF16)
    ka = proj(1)
    kn = ka * lax.rsqrt(_group_mean_sq(ka, p_ref) + EPS) * gk_ref[...]
    ka_ref[...] = kn
    kab_ref[...] = kn.astype(BF16)
    va = proj(2)
    va_ref[...] = va
    vab_ref[...] = va.astype(BF16)
    qh_ref[...] = proj(3)
    fh_ref[...] = proj(4)
    ih_ref[...] = proj(5)
    gh_ref[...] = proj(6)


def _in_proj(x2d, ln, w_b, gq, gk, p64, tm):
    n, d = x2d.shape
    cols = w_b.shape[1]
    row = lambda i: (i, 0)
    const = lambda i: (0, 0)
    o_spec = pl.BlockSpec((tm, HA_COLS), row)
    f32o = jax.ShapeDtypeStruct((n, HA_COLS), F32)
    b16o = jax.ShapeDtypeStruct((n, HA_COLS), BF16)
    return pl.pallas_call(
        _in_proj_kernel,
        grid=(n // tm,),
        in_specs=[pl.BlockSpec((tm, d), row), pl.BlockSpec((1, d), const), pl.BlockSpec((d, cols), const),
                  pl.BlockSpec((1, HA_COLS), const), pl.BlockSpec((1, HA_COLS), const),
                  pl.BlockSpec((HA_COLS, HA_COLS), const)],
        out_specs=[o_spec] * 9,
        out_shape=[b16o, f32o, f32o, b16o, b16o, f32o, f32o, f32o, f32o],
        compiler_params=_cparams(("parallel",)),
        name="in_proj",
    )(x2d, ln, w_b, gq, gk, p64)


def _bias_tile_kernel(rb_ref, o_ref, *, tb):
    h = pl.program_id(0)
    d = pl.program_id(1)
    r = lax.broadcasted_iota(jnp.int32, (tb, tb), 0)
    c = lax.broadcasted_iota(jnp.int32, (tb, tb), 1)
    bucket = _t5_bucket(d * tb + r - c)
    acc = jnp.full((tb, tb), rb_ref[h], F32)
    for b in range(1, NUM_BUCKETS):
        acc = jnp.where(bucket == b, rb_ref[b * H_A + h], acc)
    o_ref[0, 0] = acc


def _bias_tiles(rel_bias, tb):
    return pl.pallas_call(
        functools.partial(_bias_tile_kernel, tb=tb),
        grid=(H_A, 2),
        in_specs=[pl.BlockSpec(memory_space=pltpu.SMEM)],
        out_specs=pl.BlockSpec((1, 1, tb, tb), lambda h, d: (h, d, 0, 0)),
        out_shape=jax.ShapeDtypeStruct((H_A, 2, tb, tb), F32),
        compiler_params=_cparams(("parallel", "parallel")),
        name="bias_tiles",
    )(rel_bias.reshape(-1))


def _prompt_attn_kernel(lam_ref, rb_ref, q_ref, k_ref, v_ref, bias_ref, g_ref, o_ref, *, tq, out_scale):
    hps = PROMPT_HEADS_PER_STEP
    hg = pl.program_id(1)
    i = pl.program_id(2)
    lane = lax.broadcasted_iota(jnp.int32, (tq, 2 * DH), 1)
    q2s, fars = [], []
    for hh in range(hps):
        q = q_ref[0, :, hh * 2 * DH:(hh + 1) * 2 * DH]
        zero = jnp.zeros_like(q)
        q2s.append(jnp.concatenate([jnp.where(lane < DH, q, zero), jnp.where(lane >= DH, q, zero)], axis=0))
        fars.append(rb_ref[(NUM_BUCKETS - 1) * H_A + hg * hps + hh])

    def update(carry, j, s_fns):
        rows = pl.ds(pl.multiple_of(j * tq, tq), tq)
        out = []
        for hh in range(hps):
            m, l, acc = carry[hh]
            kj = k_ref[0, rows, hh * 2 * DH:(hh + 1) * 2 * DH]
            vj = v_ref[0, rows, hh * VD:(hh + 1) * VD]
            s = s_fns[hh](_dot_nt(q2s[hh], kj))
            m_new = jnp.maximum(m, jnp.max(s, axis=-1, keepdims=True))
            alpha = jnp.exp(m - m_new)
            p = jnp.exp(s - m_new)
            l = alpha * l + jnp.sum(p, axis=-1, keepdims=True)
            acc = alpha * acc + jnp.dot(p.astype(BF16), vj, preferred_element_type=F32)
            out.append((m_new, l, acc))
        return tuple(out)

    init = tuple((jnp.full((2 * tq, 1), NEG, F32), jnp.zeros((2 * tq, 1), F32), jnp.zeros((2 * tq, VD), F32))
                 for _ in range(hps))
    far_fns = [lambda s, fb=fb: s + fb for fb in fars]
    carry = lax.fori_loop(0, jnp.maximum(i - 1, 0), lambda j, c: update(c, j, far_fns), init)
    near_fns = []
    for hh in range(hps):
        b1 = bias_ref[hh, 1]
        b1 = jnp.concatenate([b1, b1], axis=0)
        near_fns.append(lambda s, b1=b1: jnp.where(i >= 1, s + b1, NEG))
    carry = update(carry, jnp.maximum(i - 1, 0), near_fns)
    r = lax.broadcasted_iota(jnp.int32, (tq, tq), 0)
    c = lax.broadcasted_iota(jnp.int32, (tq, tq), 1)
    diag_fns = []
    for hh in range(hps):
        b0 = jnp.where(r >= c, bias_ref[hh, 0], NEG)
        b0 = jnp.concatenate([b0, b0], axis=0)
        diag_fns.append(lambda s, b0=b0: jnp.where(b0 > 0.5 * NEG, s + b0, NEG))
    carry = update(carry, i, diag_fns)
    for hh in range(hps):
        m, l, acc = carry[hh]
        o = acc / l
        o = o[:tq] - lam_ref[0] * o[tq:]
        o_ref[0, :, hh * VD:(hh + 1) * VD] = (_rms(o, g_ref[...]) * out_scale).astype(BF16)


def _prompt_attn(lam, rel_bias, qab, kab, vab, tiles, subln, tq, out_scale):
    b, t, _ = qab.shape
    hps = PROMPT_HEADS_PER_STEP
    blk = lambda bb, h, i: (bb, i, h)
    full = lambda bb, h, i: (bb, 0, h)
    return pl.pallas_call(
        functools.partial(_prompt_attn_kernel, tq=tq, out_scale=out_scale),
        grid=(b, H_A // hps, t // tq),
        in_specs=[pl.BlockSpec(memory_space=pltpu.SMEM), pl.BlockSpec(memory_space=pltpu.SMEM),
                  pl.BlockSpec((1, tq, hps * 2 * DH), blk), pl.BlockSpec((1, t, hps * 2 * DH), full),
                  pl.BlockSpec((1, t, hps * VD), full),
                  pl.BlockSpec((hps, 2, tq, tq), lambda bb, h, i: (h, 0, 0, 0)),
                  pl.BlockSpec((1, VD), lambda bb, h, i: (0, 0))],
        out_specs=pl.BlockSpec((1, tq, hps * VD), blk),
        out_shape=jax.ShapeDtypeStruct((b, t, H_A * VD), BF16),
        compiler_params=_cparams(("parallel", "parallel", "arbitrary")),
        name="prompt_attn",
    )(lam, rel_bias.reshape(-1), qab, kab, vab, tiles, subln)


TPAD = 8
SROWS = H_A * 2 * TPAD
NEW_PAD = 128


def _sample_attn_kernel(pt_ref, lam_ref, wq_ref, *refs, pp, n_pages, ts, out_scale):
    k_refs = refs[:pp]
    v_refs = refs[pp:2 * pp]
    (knew_ref, vnew_ref, far_ref, tbl_ref, g_ref, o_ref, m_ref, l_ref, acc_ref) = refs[2 * pp:]
    j = pl.program_id(1)
    nsteps = n_pages // pp
    past = n_pages * PAGE_SIZE

    @pl.when(j == 0)
    def _():
        m_ref[...] = jnp.full(m_ref.shape, NEG, F32)
        l_ref[...] = jnp.zeros(l_ref.shape, F32)
        acc_ref[...] = jnp.zeros(acc_ref.shape, F32)

    wq = wq_ref[0]

    def update(s, v_heads):
        m = m_ref[...]
        m_new = jnp.maximum(m, jnp.max(s, axis=1, keepdims=True))
        alpha = jnp.exp(m - m_new)
        p = jnp.exp(s - m_new)
        l_ref[...] = alpha * l_ref[...] + jnp.sum(p, axis=1, keepdims=True)
        m_ref[...] = m_new
        pb = p.astype(BF16)
        for h in range(H_A):
            rows = slice(h * 2 * TPAD, (h + 1) * 2 * TPAD)
            acc_ref[rows, :] = acc_ref[rows, :] * alpha[rows] + jnp.dot(pb[rows], v_heads[h],
                                                                        preferred_element_type=F32)

    def near_bias(rel):
        bucket = _t5_bucket(rel)
        acc = jnp.broadcast_to(tbl_ref[:, 0:1], rel.shape)
        for b in range(1, NUM_BUCKETS):
            acc = jnp.where(bucket == b, tbl_ref[:, b:b + 1], acc)
        return acc

    def tok_of_row(shape):
        return lax.broadcasted_iota(jnp.int32, shape, 0) % TPAD

    far = far_ref[...]
    s_pages = [jnp.dot(wq, k_refs[u][0, 0].astype(BF16), preferred_element_type=F32) for u in range(pp)]
    v_heads = [jnp.concatenate([v_refs[u][0, 0, pl.ds(h, PAGE_SIZE, stride=H_A), :].astype(BF16)
                                for u in range(pp)], axis=0) for h in range(H_A)]

    @pl.when(j < nsteps - 1)
    def _():
        update(jnp.concatenate(s_pages, axis=1) + far, v_heads)

    @pl.when(j == nsteps - 1)
    def _():
        shape = (SROWS, PAGE_SIZE)
        kpos = (n_pages - 1) * PAGE_SIZE + lax.broadcasted_iota(jnp.int32, shape, 1)
        rel = past + tok_of_row(shape) - kpos
        s_last = s_pages[pp - 1] + near_bias(rel)
        update(jnp.concatenate([sp + far for sp in s_pages[:pp - 1]] + [s_last], axis=1), v_heads)
        shape = (SROWS, NEW_PAD)
        kt = lax.broadcasted_iota(jnp.int32, shape, 1)
        rel = tok_of_row(shape) - kt
        s_new = jnp.dot(wq, knew_ref[0], preferred_element_type=F32) + near_bias(rel)
        s_new = jnp.where((rel >= 0) & (kt < ts), s_new, NEG)
        update(s_new, [vnew_ref[0, :, h * VD:(h + 1) * VD] for h in range(H_A)])
        o = acc_ref[...] / l_ref[...]
        lam = lam_ref[0]
        for h in range(H_A):
            r0 = h * 2 * TPAD
            oh = o[r0:r0 + TPAD] - lam * o[r0 + TPAD:r0 + 2 * TPAD]
            o_ref[0, :, h * VD:(h + 1) * VD] = (_rms(oh, g_ref[...]) * out_scale).astype(o_ref.dtype)


def _sample_attn(layer, page_table, lam, wq, cache_kt, cache_v2, knew_t, vnew, far_col, tbl, subln, pp, ts,
                 out_scale):
    db, n_pages = page_table.shape
    nsteps = n_pages // pp

    def page_map(u):
        return lambda b, j, pt: (layer, pt[b * n_pages + j * pp + u], 0, 0)

    per_b = lambda b, j, pt: (b, 0, 0)
    const = lambda b, j, pt: (0, 0)
    page_spec = [pl.BlockSpec((1, 1, HA_COLS, PAGE_SIZE), page_map(u)) for u in range(pp)]
    grid_spec = pltpu.PrefetchScalarGridSpec(
        num_scalar_prefetch=1,
        grid=(db, nsteps),
        in_specs=[pl.BlockSpec(memory_space=pltpu.SMEM), pl.BlockSpec((1, SROWS, HA_COLS), per_b)]
        + page_spec + page_spec
        + [pl.BlockSpec((1, HA_COLS, NEW_PAD), per_b), pl.BlockSpec((1, NEW_PAD, H_A * VD), per_b),
           pl.BlockSpec((SROWS, 1), const), pl.BlockSpec((SROWS, NUM_BUCKETS), const),
           pl.BlockSpec((1, VD), const)],
        out_specs=pl.BlockSpec((1, TPAD, H_A * VD), per_b),
        scratch_shapes=[pltpu.VMEM((SROWS, 1), F32), pltpu.VMEM((SROWS, 1), F32), pltpu.VMEM((SROWS, VD), F32)],
    )
    return pl.pallas_call(
        functools.partial(_sample_attn_kernel, pp=pp, n_pages=n_pages, ts=ts, out_scale=out_scale),
        grid_spec=grid_spec,
        out_shape=jax.ShapeDtypeStruct((db, TPAD, H_A * VD), BF16),
        compiler_params=_cparams(("parallel", "arbitrary")),
        name="sample_attn",
    )(page_table.reshape(-1), lam, wq, *([cache_kt] * pp), *([cache_v2] * pp), knew_t, vnew, far_col, tbl, subln)


def _hgrn_consts(chunk):
    t = np.arange(chunk)
    rows = []
    for m in HGRN_LEVELS[1:]:
        same = (t[:, None] // m) == (t[None, :] // m)
        rows.append(same & (t[None, :] <= t[:, None]))
        rows.append(same & (t[None, :] > t[:, None]))
    cum = np.concatenate(rows, axis=0).astype(np.float32)
    cum = np.concatenate([cum, cum, cum], axis=1)
    masks = []
    for m in HGRN_LEVELS[:-1]:
        bt, bs = t[:, None] // m, t[None, :] // m
        masks.append((bt == bs + 1) & (bs % 2 == 0))
    masks.append(t[:, None] == t[None, :])
    return jnp.asarray(cum, BF16), jnp.asarray(np.stack(masks).astype(np.float32))


def _hgrn_kernel(q_ref, f_ref, i_ref, g_ref, lb_ref, gain_ref, s0_ref, cum_ref, mask_ref,
                 o_ref, s_out_ref, st_ref, *, tblk, valid_len):
    c = HGRN_CHUNK
    nlev = len(HGRN_LEVELS)
    tstep = pl.program_id(1)

    @pl.when(tstep == 0)
    def _():
        for h in range(H_B):
            st_ref[h] = jnp.transpose(s0_ref[0, h])

    lb = lb_ref[...]
    cum = cum_ref[...]
    for ci in range(tblk // c):
        rows = slice(ci * c, (ci + 1) * c)
        f = lb + (1.0 - lb) * jax.nn.sigmoid(f_ref[0, rows, :])
        g = jnp.log(jnp.maximum(f, F_MIN))
        kh = 1.0 - f
        if valid_len is not None:
            pos = tstep * tblk + ci * c + lax.broadcasted_iota(jnp.int32, g.shape, 0)
            g = jnp.where(pos < valid_len, g, 0.0)
            kh = jnp.where(pos < valid_len, kh, 0.0)
        g_hi = g.astype(BF16)
        r1 = g - g_hi.astype(F32)
        g_mid = r1.astype(BF16)
        g_lo = (r1 - g_mid.astype(F32)).astype(BF16)
        ce = jnp.dot(cum, jnp.concatenate([g_hi, g_mid, g_lo], axis=0), preferred_element_type=F32)
        for h in range(H_B):
            lanes = slice(h * DK, (h + 1) * DK)
            q = q_ref[0, rows, lanes]
            k = kh[:, lanes]
            vb = i_ref[0, rows, lanes].astype(BF16)
            kb = k.astype(BF16)
            a = jnp.where(mask_ref[nlev - 1] > 0.5, _dot_nt(q.astype(BF16), kb), 0.0)
            a = a + jnp.where(mask_ref[0] > 0.5, _dot_nt((q * jnp.exp(g[:, lanes])).astype(BF16), kb), 0.0)
            for lv in range(1, nlev - 1):
                cm = ce[(2 * lv - 2) * c:(2 * lv - 1) * c, lanes]
                em = ce[(2 * lv - 1) * c:(2 * lv) * c, lanes]
                qt = (q * jnp.exp(cm)).astype(BF16)
                kt = (k * jnp.exp(em)).astype(BF16)
                a = a + jnp.where(mask_ref[lv] > 0.5, _dot_nt(qt, kt), 0.0)
            b_incl = ce[(2 * nlev - 4) * c:(2 * nlev - 3) * c, lanes]
            b_rest = ce[(2 * nlev - 3) * c:(2 * nlev - 2) * c, lanes]
            st = st_ref[h]
            o = _dot_nt((q * jnp.exp(b_incl)).astype(BF16), st.astype(BF16))
            o = o + jnp.dot(a.astype(BF16), vb, preferred_element_type=F32)
            k_end = (k * jnp.exp(b_rest)).astype(BF16)
            vt = jnp.transpose(i_ref[0, rows, lanes]).astype(BF16)
            decay_end = jnp.exp(b_incl[c - 1:c, :])
            st_ref[h] = st * decay_end + jnp.dot(vt, k_end, preferred_element_type=F32)
            gate = g_ref[0, rows, lanes]
            o = _rms(o, gain_ref[...]) * (gate * jax.nn.sigmoid(gate))
            o_ref[0, rows, lanes] = o.astype(BF16)

    @pl.when(tstep == pl.num_programs(1) - 1)
    def _():
        for h in range(H_B):
            s_out_ref[0, h] = jnp.transpose(st_ref[h])


def _hgrn(qh, fh, ih, gh, lb, gain, s0, tblk, valid_len):
    b, t, w = qh.shape
    cum, masks = _hgrn_consts(HGRN_CHUNK)
    blk = lambda bb, i: (bb, i, 0)
    const2 = lambda bb, i: (0, 0)
    spec = pl.BlockSpec((1, tblk, w), blk)
    s_spec = pl.BlockSpec((1, H_B, DK, DV), lambda bb, i: (bb, 0, 0, 0))
    return pl.pallas_call(
        functools.partial(_hgrn_kernel, tblk=tblk, valid_len=valid_len),
        grid=(b, t // tblk),
        in_specs=[spec, spec, spec, spec, pl.BlockSpec((1, w), const2), pl.BlockSpec((1, DV), const2), s_spec,
                  pl.BlockSpec(cum.shape, const2), pl.BlockSpec(masks.shape, lambda bb, i: (0, 0, 0))],
        out_specs=[spec, s_spec],
        out_shape=[jax.ShapeDtypeStruct((b, t, w), BF16), jax.ShapeDtypeStruct((b, H_B, DK, DV), F32)],
        scratch_shapes=[pltpu.VMEM((H_B, DV, DK), F32)],
        compiler_params=_cparams(("parallel", "arbitrary")),
        name="hgrn",
    )(qh, fh, ih, gh, lb, gain, s0, cum, masks)


def _memkv_kernel(x_ref, ln_ref, wk_ref, wv_ref, gk_ref, k_ref, v_ref, kb_ref, vb_ref):
    mb = _rms(x_ref[...], ln_ref[...]).astype(BF16)
    k = jnp.dot(mb, wk_ref[...], preferred_element_type=F32)
    for h in range(MH):
        lanes = slice(h * MDH, (h + 1) * MDH)
        kn = _rms(k[:, lanes], gk_ref[...])
        k_ref[:, lanes] = kn
        kb_ref[:, lanes] = kn.astype(BF16)
    v = jnp.dot(mb, wv_ref[...], preferred_element_type=F32)
    v_ref[...] = v
    vb_ref[...] = v.astype(BF16)


def _memkv(mem2d, ln, wk_b, wv_b, gk, tm):
    n, d = mem2d.shape
    w = MH * MDH
    row = lambda i: (i, 0)
    const = lambda i: (0, 0)
    o_spec = pl.BlockSpec((tm, w), row)
    return pl.pallas_call(
        _memkv_kernel,
        grid=(n // tm,),
        in_specs=[pl.BlockSpec((tm, d), row), pl.BlockSpec((1, d), const), pl.BlockSpec((d, w), const),
                  pl.BlockSpec((d, w), const), pl.BlockSpec((1, MDH), const)],
        out_specs=[o_spec] * 4,
        out_shape=[jax.ShapeDtypeStruct((n, w), F32), jax.ShapeDtypeStruct((n, w), F32),
                   jax.ShapeDtypeStruct((n, w), BF16), jax.ShapeDtypeStruct((n, w), BF16)],
        compiler_params=_cparams(("parallel",)),
        name="memkv",
    )(mem2d, ln, wk_b, wv_b, gk)


def _mid_kernel(x_ref, oa_ref, ob_ref, woa_ref, wob_ref, ln_ref, wq_ref, gq_ref, mk_ref, mv_ref, wo_ref,
                y_ref, qm_ref, om_ref, *, bb, tt):
    x = x_ref[...]
    x = x + jnp.dot(oa_ref[...], woa_ref[...], preferred_element_type=F32) + jnp.dot(
        ob_ref[...], wob_ref[...], preferred_element_type=F32)
    hb = _rms(x, ln_ref[...]).astype(BF16)
    qm = jnp.dot(hb, wq_ref[...], preferred_element_type=F32)
    for h in range(MH):
        lanes = slice(h * MDH, (h + 1) * MDH)
        qm_ref[:, lanes] = _rms(qm[:, lanes], gq_ref[...]) * (MDH ** -0.5)

    def per_batch(bi, carry):
        r = pl.ds(pl.multiple_of(bi * tt, tt), tt)
        for h in range(MH):
            lanes = slice(h * MDH, (h + 1) * MDH)
            s = _dot_nt(qm_ref[r, lanes].astype(BF16), mk_ref[bi, :, lanes])
            s = s - jnp.max(s, axis=-1, keepdims=True)
            p = jnp.exp(s)
            p = p / jnp.sum(p, axis=-1, keepdims=True)
            om_ref[r, lanes] = jnp.dot(p.astype(BF16), mv_ref[bi, :, lanes], preferred_element_type=F32)
        return carry

    lax.fori_loop(0, bb, per_batch, 0)
    y_ref[...] = x + jnp.dot(om_ref[...].astype(BF16), wo_ref[...], preferred_element_type=F32)


def _mid(x2d, oa, ob, woa, wob, ln, wq, gq, mkb, mvb, wo, t, bb, tt):
    n, d = x2d.shape
    n_mem = mkb.shape[1]
    w = H_A * VD
    assert bb == 1 or tt == t
    per_batch = t // tt
    row = lambda i: (i, 0)
    const = lambda i: (0, 0)
    mem_spec = pl.BlockSpec((bb, n_mem, MH * MDH), lambda i: (i // per_batch, 0, 0))
    rows = bb * tt
    return pl.pallas_call(
        functools.partial(_mid_kernel, bb=bb, tt=tt),
        grid=(n // rows,),
        in_specs=[pl.BlockSpec((rows, d), row), pl.BlockSpec((rows, w), row), pl.BlockSpec((rows, w), row),
                  pl.BlockSpec((w, d), const), pl.BlockSpec((w, d), const), pl.BlockSpec((1, d), const),
                  pl.BlockSpec((d, MH * MDH), const), pl.BlockSpec((1, MDH), const), mem_spec, mem_spec,
                  pl.BlockSpec((MH * MDH, d), const)],
        out_specs=pl.BlockSpec((rows, d), row),
        out_shape=jax.ShapeDtypeStruct((n, d), F32),
        scratch_shapes=[pltpu.VMEM((rows, MH * MDH), F32), pltpu.VMEM((rows, MH * MDH), F32)],
        compiler_params=_cparams(("parallel",)),
        name="mid",
    )(x2d, oa, ob, woa, wob, ln, wq, gq, mkb, mvb, wo)


LANE = 128


SUBLANES = 8


def _sort_network(n):
    pairs = []
    p = 1
    while p < n:
        k = p
        while k >= 1:
            for j in range(k % p, n - k, 2 * k):
                for i in range(min(k, n - j - k)):
                    if (i + j) // (2 * p) == (i + j + k) // (2 * p):
                        pairs.append((i + j, i + j + k))
            k //= 2
        p *= 2
    return pairs


def _sorted_tiles(tiles):
    wires = list(tiles) + [None] * (PEER_TOPK - len(tiles))
    for a, b in _sort_network(PEER_TOPK):
        if wires[a] is None:
            wires[a], wires[b] = wires[b], None
        elif wires[b] is not None:
            wires[a], wires[b] = jnp.maximum(wires[a], wires[b]), jnp.minimum(wires[a], wires[b])
    return [w for w in wires if w is not None]


def _largest_of_sorted(vs, count):
    out = []
    for r in range(count):
        m = jnp.max(vs[0], axis=0, keepdims=True)
        out.append(m)
        need = count - r - 1
        if need == 0:
            break
        eq = vs[0] == m
        vs = [jnp.where(eq, vs[i + 1] if i + 1 < len(vs) else -jnp.inf, vs[i]) for i in range(min(len(vs), need))]
    return out


def _split_tiles(x):
    return [x[i * SUBLANES:(i + 1) * SUBLANES] for i in range(x.shape[0] // SUBLANES)]


def _count_leading(rows_ref, pred):
    cnt = None
    for r in range(PEER_TOPK):
        hit = pred(rows_ref[r:r + 1, :])
        cnt = jnp.where(hit, float(r + 1), 0.0 if cnt is None else cnt)
    return cnt


def _peer_kernel(x_ref, ln_ref, wqt_ref, sk_ref, *refs, ec, n_chunks):
    u_refs = refs[:PEER_DMA_SPLIT]
    vtp_refs = refs[PEER_DMA_SPLIT:2 * PEER_DMA_SPLIT]
    (vtl_ref, y_ref, ht_ref, qt_ref, n_ref, e1_ref, r2_ref, w2_ref, acc_ref, s1_ref, s2_ref, v1_ref, v2_ref,
     wa_ref, wb_ref) = refs[2 * PEER_DMA_SPLIT:]
    cidx = pl.program_id(1)
    tt = x_ref.shape[0]

    @pl.when(cidx == 0)
    def _():
        hn = _rms(x_ref[...], ln_ref[...])
        ht_ref[...] = jnp.transpose(hn).astype(BF16)
        qt_ref[...] = jnp.dot(wqt_ref[...], ht_ref[...], preferred_element_type=F32).astype(BF16)
        acc_ref[...] = jnp.zeros(acc_ref.shape, F32)
        wb_ref[...] = jnp.zeros(wb_ref.shape, BF16)

        def per_head(hd, carry):
            def scores(p):
                r0 = pl.multiple_of((hd * 2 + p) * N_KEYS, N_KEYS)
                return jnp.dot(sk_ref[hd * 2 + p], qt_ref[pl.ds(r0, N_KEYS), :], preferred_element_type=F32)

            s1_ref[...] = scores(0)
            s2_ref[...] = scores(1)

            def per_lane_tile(lt, carry2):
                lanes = pl.ds(pl.multiple_of(lt * LANE, LANE), LANE)
                s1 = s1_ref[:, lanes]
                s2 = s2_ref[:, lanes]
                top1 = _largest_of_sorted(_sorted_tiles(_split_tiles(s1)), PEER_TOPK)
                top2 = _largest_of_sorted(_sorted_tiles(_split_tiles(s2)), PEER_TOPK)
                for r in range(PEER_TOPK):
                    v1_ref[r:r + 1, :] = top1[r]
                    v2_ref[r:r + 1, :] = top2[r]
                v2_lo = v2_ref[0:SUBLANES, :]
                cand = ([top1[0] + v2_lo, top1[0] + v2_ref[SUBLANES:PEER_TOPK, :]]
                        + [top1[i] + v2_lo for i in range(1, SUBLANES)]
                        + [v1_ref[SUBLANES:PEER_TOPK, :] + top2[0]])
                tau = _largest_of_sorted(_sorted_tiles(cand), PEER_TOPK)[PEER_TOPK - 1]
                top = top1[0] + top2[0]
                zsum = None
                for c in cand:
                    term = jnp.where(c >= tau, jnp.exp(c - top), 0.0)
                    zsum = term if zsum is None else zsum + term
                z = jnp.sum(zsum, axis=0, keepdims=True)
                th = tau - s1
                n_ref[hd, :, lanes] = _count_leading(v2_ref, lambda row: row >= th)
                e1_ref[hd, :, lanes] = jnp.exp(s1 - top1[0])
                r2_ref[hd, :, lanes] = _count_leading(v2_ref, lambda row: row > s2).astype(BF16)
                w2_ref[hd, :, lanes] = (jnp.exp(s2 - top2[0]) / z).astype(BF16)
                return carry2

            lax.fori_loop(0, tt // LANE, per_lane_tile, 0)
            return carry

        lax.fori_loop(0, PEER_HEADS, per_head, 0)

    n_a = ec // N_KEYS

    def step(w_cur_ref, w_prev_ref):
        vt_prev = jnp.concatenate([r[...] for r in vtp_refs], axis=0)
        acc_ref[...] += jnp.dot(vt_prev, w_prev_ref[...], preferred_element_type=F32)
        u_cur = jnp.concatenate([r[...] for r in u_refs], axis=0)
        act_all = jnp.dot(u_cur, ht_ref[...], preferred_element_type=F32)
        zero = jnp.zeros((N_KEYS, LANE), BF16)
        a0 = pl.multiple_of(cidx * n_a, n_a)
        n_rows = [n_ref[hd, pl.ds(a0, n_a), :] for hd in range(PEER_HEADS)]
        e1_rows = [e1_ref[hd, pl.ds(a0, n_a), :] for hd in range(PEER_HEADS)]
        for aa in range(n_a):
            for lt in range(tt // LANE):
                lanes = slice(lt * LANE, (lt + 1) * LANE)
                act = act_all[aa * N_KEYS:(aa + 1) * N_KEYS, lanes]
                act = 0.5 * act * (1.0 + lax.erf(act * SQRT_HALF))
                g = zero
                for hd in range(PEER_HEADS):
                    nb = jnp.broadcast_to(n_rows[hd][aa:aa + 1, lanes], (N_KEYS, LANE)).astype(BF16)
                    eb = jnp.broadcast_to(e1_rows[hd][aa:aa + 1, lanes], (N_KEYS, LANE)).astype(BF16)
                    g = g + jnp.where(r2_ref[hd, :, lanes] < nb, w2_ref[hd, :, lanes] * eb, zero)
                w_cur_ref[aa * N_KEYS:(aa + 1) * N_KEYS, lanes] = g * act.astype(BF16)

    @pl.when(cidx % 2 == 0)
    def _():
        step(wa_ref, wb_ref)

    @pl.when(cidx % 2 == 1)
    def _():
        step(wb_ref, wa_ref)

    @pl.when(cidx == n_chunks - 1)
    def _():
        w_last_ref = wa_ref if (n_chunks - 1) % 2 == 0 else wb_ref
        acc = acc_ref[...] + jnp.dot(vtl_ref[...], w_last_ref[...], preferred_element_type=F32)
        y_ref[...] = x_ref[...] + jnp.transpose(acc)


def _peer(x2d, ln, wqt_b, sk_b, u_b, vt_b, tt, ec):
    n, d = x2d.shape
    n_exp = u_b.shape[0]
    assert tt % LANE == 0 and ec % (8 * N_KEYS) == 0
    n_chunks = n_exp // ec
    row = lambda i, c: (i, 0)
    const = lambda i, c: (0, 0)
    head_f32 = pltpu.VMEM((PEER_HEADS, N_KEYS, tt), F32)
    head_b16 = pltpu.VMEM((PEER_HEADS, N_KEYS, tt), BF16)
    ns = PEER_DMA_SPLIT
    u_specs = [pl.BlockSpec((ec // ns, d), lambda i, c, k=k: (c * ns + k, 0)) for k in range(ns)]
    vtp_specs = [pl.BlockSpec((d // ns, ec), lambda i, c, k=k: (k, jnp.maximum(c - 1, 0))) for k in range(ns)]
    return pl.pallas_call(
        functools.partial(_peer_kernel, ec=ec, n_chunks=n_chunks),
        grid=(n // tt, n_chunks),
        in_specs=[pl.BlockSpec((tt, d), row), pl.BlockSpec((1, d), const), pl.BlockSpec(wqt_b.shape, const),
                  pl.BlockSpec(sk_b.shape, lambda i, c: (0, 0, 0))] + u_specs + vtp_specs
        + [pl.BlockSpec((d, ec), lambda i, c: (0, n_chunks - 1))],
        out_specs=pl.BlockSpec((tt, d), row),
        out_shape=jax.ShapeDtypeStruct((n, d), F32),
        scratch_shapes=[pltpu.VMEM((d, tt), BF16), pltpu.VMEM((wqt_b.shape[0], tt), BF16),
                        head_f32, head_f32, head_b16, head_b16,
                        pltpu.VMEM((d, tt), F32), pltpu.VMEM((N_KEYS, tt), F32), pltpu.VMEM((N_KEYS, tt), F32),
                        pltpu.VMEM((PEER_TOPK, LANE), F32), pltpu.VMEM((PEER_TOPK, LANE), F32),
                        pltpu.VMEM((ec, tt), BF16), pltpu.VMEM((ec, tt), BF16)],
        compiler_params=_cparams(("parallel", "arbitrary")),
        name="peer",
    )(x2d, ln, wqt_b, sk_b, *([u_b] * ns), *([vt_b] * ns), vt_b)


def _tile(n, pref):
    t = min(n, pref)
    while n % t:
        t //= 2
    return t


def _pad_axis(a, axis, size):
    pad = [(0, 0)] * a.ndim
    pad[axis] = (0, size - a.shape[axis])
    return jnp.pad(a, pad)


def kernel(x_prompt, x_sample, cache_k, cache_v, cache_mem_k, cache_mem_v, state_hgrn, page_table, mem_prompt,
           ln_mix, w_in, qk_gain, lambda_qk, subln_gain, rel_bias, lower_bounds, hgrn_norm, w_out, ln_mem,
           mem_norm, w_mq, w_mk, w_mv, mem_qk_gain, w_mo, ln_ffn, peer_wq, peer_subkeys, peer_u, peer_v):
    B, T, D = x_prompt.shape
    DB, TS, _ = x_sample.shape
    depth = ln_mix.shape[0]
    n_pages = page_table.shape[1]
    n_mem = mem_prompt.shape[1]
    n_phys = cache_k.shape[1]
    W = HA_COLS

    lbp = jax.nn.softmax(lower_bounds.astype(F32), axis=0)
    lbs = jnp.cumsum(lbp, axis=0) - lbp[0]
    p64 = jnp.asarray(np.kron(np.eye(W // DH), np.full((DH, DH), 1.0 / DH)), BF16)
    tq = _tile(T, 256)
    tiles = _bias_tiles(rel_bias, tq)
    cache_kt = jnp.transpose(cache_k, (0, 1, 3, 4, 5, 2)).reshape(depth, n_phys, W, PAGE_SIZE)
    cache_v2 = cache_v.reshape(depth, n_phys, PAGE_SIZE * H_A, VD)
    row_head = np.arange(SROWS) // (2 * TPAD)
    tbl = rel_bias[:, row_head].T
    far_col = tbl[:, NUM_BUCKETS - 1:]
    grp_mask = jnp.asarray((np.arange(W)[None, :] // DH == np.arange(W // DH)[:, None]).astype(np.float32), BF16)

    ts_pad = 8
    yp, ys = x_prompt, x_sample
    outs = {k: [] for k in ("kp", "vp", "mkp", "mvp", "sp", "ks", "vs", "ss")}
    for l in range(depth):
        lambda_init = 0.8 - 0.6 * math.exp(-0.3 * l)
        out_scale = 1.0 - lambda_init
        lq1, lk1, lq2, lk2 = lambda_qk[l].astype(F32)
        lam = (jnp.exp(jnp.sum(lq1 * lk1)) - jnp.exp(jnp.sum(lq2 * lk2)) + lambda_init).reshape(1)
        w_in_b = w_in[l].astype(BF16)
        gq = jnp.tile(qk_gain[l, 0], W // DH).reshape(1, W)
        gk = jnp.tile(qk_gain[l, 1], W // DH).reshape(1, W)
        ln_mix_l = ln_mix[l].reshape(1, D)
        subln = subln_gain[l].reshape(1, VD)
        lb = lbs[l].reshape(1, H_B * DK)
        hgain = hgrn_norm[l].reshape(1, DV)
        woa = w_out[l, :H_A * VD].astype(BF16)
        wob = w_out[l, H_A * VD:].astype(BF16)
        ln_mem_l = ln_mem[l].reshape(1, D)
        wmq = w_mq[l].astype(BF16)
        gmq = mem_qk_gain[l, 0].reshape(1, MDH)
        gmk = mem_qk_gain[l, 1].reshape(1, MDH)
        wmo = w_mo[l].astype(BF16)
        ln_ffn_l = ln_ffn[l].reshape(1, D)
        wqt = peer_wq[l].T.astype(BF16)
        sk = peer_subkeys[l].reshape(PEER_HEADS * 2, N_KEYS, -1).astype(BF16)
        u_b = peer_u[l].astype(BF16)
        vt_b = peer_v[l].T.astype(BF16)

        mk, mv, mkb, mvb = _memkv(mem_prompt.reshape(B * n_mem, D), mem_norm[l].reshape(1, D),
                                  w_mk[l].astype(BF16), w_mv[l].astype(BF16), gmk, _tile(B * n_mem, 512))
        qab, ka, va, kab, vab, qh, fh, ih, gh = _in_proj(yp.reshape(B * T, D), ln_mix_l, w_in_b, gq, gk, p64,
                                                         _tile(B * T, 512))
        r3 = lambda a: a.reshape(B, T, W)
        oa = _prompt_attn(lam, rel_bias, r3(qab), r3(kab), r3(vab), tiles, subln, tq, out_scale)
        ob, sp = _hgrn(r3(qh), r3(fh), r3(ih), r3(gh), lb, hgain, jnp.zeros((B, H_B, DK, DV), F32),
                       _tile(T, 256), None)
        x2 = _mid(yp.reshape(B * T, D), oa.reshape(B * T, W), ob.reshape(B * T, W), woa, wob, ln_mem_l, wmq, gmq,
                  mkb.reshape(B, n_mem, W), mvb.reshape(B, n_mem, W), wmo, T, 1, _tile(T, 512))
        yp = _peer(x2, ln_ffn_l, wqt, sk, u_b, vt_b, _tile(B * T, 512), PEER_EC).reshape(B, T, D)
        outs["kp"].append(ka.reshape(B, T, H_A, 2, DH))
        outs["vp"].append(va.reshape(B, T, H_A, VD))
        outs["mkp"].append(mk.reshape(B, n_mem, MH, MDH))
        outs["mvp"].append(mv.reshape(B, n_mem, MH, MDH))
        outs["sp"].append(sp)

        qab, ka, va, kab, vab, qh, fh, ih, gh = _in_proj(ys.reshape(DB * TS, D), ln_mix_l, w_in_b, gq, gk, p64,
                                                         _tile(DB * TS, 512))
        s3 = lambda a: a.reshape(DB, TS, W)
        qs = _pad_axis(s3(qab), 1, TPAD)
        wq_s = (qs[:, None, :, :] * grp_mask[None, :, None, :]).reshape(DB, SROWS, W)
        knew_t = _pad_axis(jnp.swapaxes(s3(kab), 1, 2), 2, NEW_PAD)
        vnew = _pad_axis(s3(vab), 1, NEW_PAD)
        oa_s = _sample_attn(l, page_table, lam, wq_s, cache_kt, cache_v2, knew_t, vnew, far_col, tbl, subln,
                            _tile(n_pages, SAMPLE_PAGES_PER_STEP), TS, out_scale)
        hp = lambda a: _pad_axis(s3(a), 1, HGRN_CHUNK)
        ob_s, ss = _hgrn(hp(qh), hp(fh), hp(ih), hp(gh), lb, hgain, state_hgrn[l].astype(F32), HGRN_CHUNK, TS)
        x2 = _mid(_pad_axis(ys, 1, ts_pad).reshape(DB * ts_pad, D), oa_s.reshape(DB * ts_pad, W),
                  ob_s[:, :ts_pad].reshape(DB * ts_pad, W), woa, wob, ln_mem_l, wmq, gmq,
                  cache_mem_k[l].reshape(DB, n_mem, W).astype(BF16),
                  cache_mem_v[l].reshape(DB, n_mem, W).astype(BF16), wmo, ts_pad, DB, ts_pad)
        ys = _peer(x2, ln_ffn_l, wqt, sk, u_b, vt_b, DB * ts_pad, PEER_EC).reshape(DB, ts_pad, D)[:, :TS]
        outs["ks"].append(ka.reshape(DB, TS, H_A, 2, DH))
        outs["vs"].append(va.reshape(DB, TS, H_A, VD))
        outs["ss"].append(ss.astype(state_hgrn.dtype))

    st = lambda k: jnp.stack(outs[k])
    return (yp, ys, st("kp"), st("vp"), st("mkp"), st("mvp"), st("sp"), st("ks"), st("vs"), st("ss"))
```

```python
import functools
import math

import numpy as np
import jax
import jax.numpy as jnp
from jax import lax
from jax.experimental import pallas as pl
from jax.experimental.pallas import tpu as pltpu

F32 = jnp.float32
BF16 = jnp.bfloat16

PAGE_SIZE = 128
H_A = 4
DH = 64
VD = 2 * DH
H_B = 4
DK = 128
DV = 128
NUM_BUCKETS = 32
MAX_DISTANCE = 128
MH = 4
MDH = 128
PEER_HEADS = 8
N_KEYS = 128
PEER_TOPK = 16
EPS = 1e-6
NEG = -1e30
F_MIN = 1e-20
HA_COLS = H_A * 2 * DH
SQRT_HALF = 0.7071067811865476

VMEM_LIMIT_BYTES = 56 * 1024 * 1024
HGRN_CHUNK = 64
HGRN_LEVELS = (1, 2, 4, 8, 16, 32, 64)
PEER_EC = 1024
PEER_DMA_SPLIT = 4
SAMPLE_PAGES_PER_STEP = 16
PROMPT_HEADS_PER_STEP = 2


def _cparams(sem):
    return pltpu.CompilerParams(dimension_semantics=sem, vmem_limit_bytes=VMEM_LIMIT_BYTES)


def _rms(x, g):
    return x * lax.rsqrt(jnp.mean(x * x, axis=-1, keepdims=True) + EPS) * g


def _group_mean_sq(x, p_ref):
    sq = x * x
    hi = sq.astype(BF16)
    lo = (sq - hi.astype(F32)).astype(BF16)
    p = p_ref[...]
    return jnp.dot(hi, p, preferred_element_type=F32) + jnp.dot(lo, p, preferred_element_type=F32)


def _dot_nt(a, b):
    return lax.dot_general(a, b, (((1,), (1,)), ((), ())), preferred_element_type=F32)


def _t5_bucket(rel):
    n = jnp.maximum(rel, 0)
    max_exact = NUM_BUCKETS // 2
    nf = jnp.maximum(n, max_exact).astype(F32)
    large = max_exact + (jnp.log(nf / max_exact) / math.log(MAX_DISTANCE / max_exact)
                         * (NUM_BUCKETS - max_exact)).astype(jnp.int32)
    large = jnp.minimum(large, NUM_BUCKETS - 1)
    return jnp.where(n < max_exact, n, large)


def _in_proj_kernel(x_ref, ln_ref, w_ref, gq_ref, gk_ref, p_ref,
                    qab_ref, ka_ref, va_ref, kab_ref, vab_ref, qh_ref, fh_ref, ih_ref, gh_ref, *, k_transposed):
    hb = _rms(x_ref[...], ln_ref[...]).astype(BF16)

    def proj(j):
        return jnp.dot(hb, w_ref[:, j * HA_COLS:(j + 1) * HA_COLS], preferred_element_type=F32)

    qa = proj(0)
    qn = qa * lax.rsqrt(_group_mean_sq(qa, p_ref) + EPS) * gq_ref[...]
    qab_ref[...] = (qn * (DH ** -0.5)).astype(BF16)
    ka = proj(1)
    kn = ka * lax.rsqrt(_group_mean_sq(ka, p_ref) + EPS) * gk_ref[...]
    if k_transposed:
        ka_ref[0] = jnp.transpose(kn)
    else:
        ka_ref[...] = kn
    kab_ref[...] = kn.astype(BF16)
    va = proj(2)
    va_ref[...] = va
    vab_ref[...] = va.astype(BF16)
    qh_ref[...] = proj(3)
    fh_ref[...] = proj(4)
    ih_ref[...] = proj(5)
    gh_ref[...] = proj(6)


def _in_proj(x2d, ln, w_b, gq, gk, p64, tm, seq_len=None):
    n, d = x2d.shape
    cols = w_b.shape[1]
    row = lambda i: (i, 0)
    const = lambda i: (0, 0)
    o_spec = pl.BlockSpec((tm, HA_COLS), row)
    f32o = jax.ShapeDtypeStruct((n, HA_COLS), F32)
    b16o = jax.ShapeDtypeStruct((n, HA_COLS), BF16)
    k_spec, k_shape = o_spec, f32o
    if seq_len is not None:
        per_seq = seq_len // tm
        k_spec = pl.BlockSpec((1, HA_COLS, tm), lambda i: (i // per_seq, 0, i % per_seq))
        k_shape = jax.ShapeDtypeStruct((n // seq_len, HA_COLS, seq_len), F32)
    return pl.pallas_call(
        functools.partial(_in_proj_kernel, k_transposed=seq_len is not None),
        grid=(n // tm,),
        in_specs=[pl.BlockSpec((tm, d), row), pl.BlockSpec((1, d), const), pl.BlockSpec((d, cols), const),
                  pl.BlockSpec((1, HA_COLS), const), pl.BlockSpec((1, HA_COLS), const),
                  pl.BlockSpec((HA_COLS, HA_COLS), const)],
        out_specs=[o_spec, k_spec] + [o_spec] * 7,
        out_shape=[b16o, k_shape, f32o, b16o, b16o, f32o, f32o, f32o, f32o],
        compiler_params=_cparams(("parallel",)),
        name="in_proj",
    )(x2d, ln, w_b, gq, gk, p64)


def _bias_tile_kernel(rb_ref, o_ref, *, tb):
    h = pl.program_id(0)
    d = pl.program_id(1)
    r = lax.broadcasted_iota(jnp.int32, (tb, tb), 0)
    c = lax.broadcasted_iota(jnp.int32, (tb, tb), 1)
    bucket = _t5_bucket(d * tb + r - c)
    acc = jnp.full((tb, tb), rb_ref[h], F32)
    for b in range(1, NUM_BUCKETS):
        acc = jnp.where(bucket == b, rb_ref[b * H_A + h], acc)
    o_ref[0, 0] = acc


def _bias_tiles(rel_bias, tb):
    return pl.pallas_call(
        functools.partial(_bias_tile_kernel, tb=tb),
        grid=(H_A, 2),
        in_specs=[pl.BlockSpec(memory_space=pltpu.SMEM)],
        out_specs=pl.BlockSpec((1, 1, tb, tb), lambda h, d: (h, d, 0, 0)),
        out_shape=jax.ShapeDtypeStruct((H_A, 2, tb, tb), F32),
        compiler_params=_cparams(("parallel", "parallel")),
        name="bias_tiles",
    )(rel_bias.reshape(-1))


def _prompt_attn_kernel(lam_ref, rb_ref, q_ref, k_ref, v_ref, bias_ref, g_ref, o_ref, *, tq, out_scale):
    hps = PROMPT_HEADS_PER_STEP
    hg = pl.program_id(1)
    i = pl.program_id(2)
    lane = lax.broadcasted_iota(jnp.int32, (tq, 2 * DH), 1)
    q2s, fars = [], []
    for hh in range(hps):
        q = q_ref[0, :, hh * 2 * DH:(hh + 1) * 2 * DH]
        zero = jnp.zeros_like(q)
        q2s.append(jnp.concatenate([jnp.where(lane < DH, q, zero), jnp.where(lane >= DH, q, zero)], axis=0))
        fars.append(rb_ref[(NUM_BUCKETS - 1) * H_A + hg * hps + hh])

    def update(carry, j, s_fns):
        rows = pl.ds(pl.multiple_of(j * tq, tq), tq)
        out = []
        for hh in range(hps):
            m, l, acc = carry[hh]
            kj = k_ref[0, rows, hh * 2 * DH:(hh + 1) * 2 * DH]
            vj = v_ref[0, rows, hh * VD:(hh + 1) * VD]
            s = s_fns[hh](_dot_nt(q2s[hh], kj))
            m_new = jnp.maximum(m, jnp.max(s, axis=-1, keepdims=True))
            alpha = jnp.exp(m - m_new)
            p = jnp.exp(s - m_new)
            l = alpha * l + jnp.sum(p, axis=-1, keepdims=True)
            acc = alpha * acc + jnp.dot(p.astype(BF16), vj, preferred_element_type=F32)
            out.append((m_new, l, acc))
        return tuple(out)

    init = tuple((jnp.full((2 * tq, 1), NEG, F32), jnp.zeros((2 * tq, 1), F32), jnp.zeros((2 * tq, VD), F32))
                 for _ in range(hps))
    far_fns = [lambda s, fb=fb: s + fb for fb in fars]
    carry = lax.fori_loop(0, jnp.maximum(i - 1, 0), lambda j, c: update(c, j, far_fns), init)
    near_fns = []
    for hh in range(hps):
        b1 = bias_ref[hh, 1]
        b1 = jnp.concatenate([b1, b1], axis=0)
        near_fns.append(lambda s, b1=b1: jnp.where(i >= 1, s + b1, NEG))
    carry = update(carry, jnp.maximum(i - 1, 0), near_fns)
    r = lax.broadcasted_iota(jnp.int32, (tq, tq), 0)
    c = lax.broadcasted_iota(jnp.int32, (tq, tq), 1)
    diag_fns = []
    for hh in range(hps):
        b0 = jnp.where(r >= c, bias_ref[hh, 0], NEG)
        b0 = jnp.concatenate([b0, b0], axis=0)
        diag_fns.append(lambda s, b0=b0: jnp.where(b0 > 0.5 * NEG, s + b0, NEG))
    carry = update(carry, i, diag_fns)
    for hh in range(hps):
        m, l, acc = carry[hh]
        o = acc / l
        o = o[:tq] - lam_ref[0] * o[tq:]
        o_ref[0, :, hh * VD:(hh + 1) * VD] = (_rms(o, g_ref[...]) * out_scale).astype(BF16)


def _prompt_attn(lam, rel_bias, qab, kab, vab, tiles, subln, tq, out_scale):
    b, t, _ = qab.shape
    hps = PROMPT_HEADS_PER_STEP
    blk = lambda bb, h, i: (bb, i, h)
    full = lambda bb, h, i: (bb, 0, h)
    return pl.pallas_call(
        functools.partial(_prompt_attn_kernel, tq=tq, out_scale=out_scale),
        grid=(b, H_A // hps, t // tq),
        in_specs=[pl.BlockSpec(memory_space=pltpu.SMEM), pl.BlockSpec(memory_space=pltpu.SMEM),
                  pl.BlockSpec((1, tq, hps * 2 * DH), blk), pl.BlockSpec((1, t, hps * 2 * DH), full),
                  pl.BlockSpec((1, t, hps * VD), full),
                  pl.BlockSpec((hps, 2, tq, tq), lambda bb, h, i: (h, 0, 0, 0)),
                  pl.BlockSpec((1, VD), lambda bb, h, i: (0, 0))],
        out_specs=pl.BlockSpec((1, tq, hps * VD), blk),
        out_shape=jax.ShapeDtypeStruct((b, t, H_A * VD), BF16),
        compiler_params=_cparams(("parallel", "parallel", "arbitrary")),
        name="prompt_attn",
    )(lam, rel_bias.reshape(-1), qab, kab, vab, tiles, subln)


TPAD = 8
SROWS = H_A * 2 * TPAD
NEW_PAD = 128


def _sample_attn_kernel(pt_ref, lam_ref, wq_ref, *refs, pp, n_pages, ts, out_scale):
    k_refs = refs[:pp]
    v_refs = refs[pp:2 * pp]
    (knew_ref, vnew_ref, far_ref, tbl_ref, g_ref, o_ref, m_ref, l_ref, acc_ref) = refs[2 * pp:]
    j = pl.program_id(1)
    nsteps = n_pages // pp
    past = n_pages * PAGE_SIZE

    @pl.when(j == 0)
    def _():
        m_ref[...] = jnp.full(m_ref.shape, NEG, F32)
        l_ref[...] = jnp.zeros(l_ref.shape, F32)
        acc_ref[...] = jnp.zeros(acc_ref.shape, F32)

    wq = wq_ref[0]

    def update(s, v_heads):
        m = m_ref[...]
        m_new = jnp.maximum(m, jnp.max(s, axis=1, keepdims=True))
        alpha = jnp.exp(m - m_new)
        p = jnp.exp(s - m_new)
        l_ref[...] = alpha * l_ref[...] + jnp.sum(p, axis=1, keepdims=True)
        m_ref[...] = m_new
        pb = p.astype(BF16)
        for h in range(H_A):
            rows = slice(h * 2 * TPAD, (h + 1) * 2 * TPAD)
            acc_ref[rows, :] = acc_ref[rows, :] * alpha[rows] + jnp.dot(pb[rows], v_heads[h],
                                                                        preferred_element_type=F32)

    def near_bias(rel):
        bucket = _t5_bucket(rel)
        acc = jnp.broadcast_to(tbl_ref[:, 0:1], rel.shape)
        for b in range(1, NUM_BUCKETS):
            acc = jnp.where(bucket == b, tbl_ref[:, b:b + 1], acc)
        return acc

    def tok_of_row(shape):
        return lax.broadcasted_iota(jnp.int32, shape, 0) % TPAD

    far = far_ref[...]
    s_pages = [jnp.dot(wq, k_refs[u][0, 0].astype(BF16), preferred_element_type=F32) for u in range(pp)]
    v_heads = [jnp.concatenate([v_refs[u][0, 0, pl.ds(h, PAGE_SIZE, stride=H_A), :].astype(BF16)
                                for u in range(pp)], axis=0) for h in range(H_A)]

    @pl.when(j < nsteps - 1)
    def _():
        update(jnp.concatenate(s_pages, axis=1) + far, v_heads)

    @pl.when(j == nsteps - 1)
    def _():
        shape = (SROWS, PAGE_SIZE)
        kpos = (n_pages - 1) * PAGE_SIZE + lax.broadcasted_iota(jnp.int32, shape, 1)
        rel = past + tok_of_row(shape) - kpos
        s_last = s_pages[pp - 1] + near_bias(rel)
        update(jnp.concatenate([sp + far for sp in s_pages[:pp - 1]] + [s_last], axis=1), v_heads)
        shape = (SROWS, NEW_PAD)
        kt = lax.broadcasted_iota(jnp.int32, shape, 1)
        rel = tok_of_row(shape) - kt
        s_new = jnp.dot(wq, knew_ref[0], preferred_element_type=F32) + near_bias(rel)
        s_new = jnp.where((rel >= 0) & (kt < ts), s_new, NEG)
        update(s_new, [vnew_ref[0, :, h * VD:(h + 1) * VD] for h in range(H_A)])
        o = acc_ref[...] / l_ref[...]
        lam = lam_ref[0]
        for h in range(H_A):
            r0 = h * 2 * TPAD
            oh = o[r0:r0 + TPAD] - lam * o[r0 + TPAD:r0 + 2 * TPAD]
            o_ref[0, :, h * VD:(h + 1) * VD] = (_rms(oh, g_ref[...]) * out_scale).astype(o_ref.dtype)


def _sample_attn(layer, page_table, lam, wq, cache_kt, cache_v2, knew_t, vnew, far_col, tbl, subln, pp, ts,
                 out_scale):
    db, n_pages = page_table.shape
    nsteps = n_pages // pp

    def page_map(u):
        return lambda b, j, pt: (layer, pt[b * n_pages + j * pp + u], 0, 0)

    per_b = lambda b, j, pt: (b, 0, 0)
    const = lambda b, j, pt: (0, 0)
    page_spec = [pl.BlockSpec((1, 1, HA_COLS, PAGE_SIZE), page_map(u)) for u in range(pp)]
    grid_spec = pltpu.PrefetchScalarGridSpec(
        num_scalar_prefetch=1,
        grid=(db, nsteps),
        in_specs=[pl.BlockSpec(memory_space=pltpu.SMEM), pl.BlockSpec((1, SROWS, HA_COLS), per_b)]
        + page_spec + page_spec
        + [pl.BlockSpec((1, HA_COLS, NEW_PAD), per_b), pl.BlockSpec((1, NEW_PAD, H_A * VD), per_b),
           pl.BlockSpec((SROWS, 1), const), pl.BlockSpec((SROWS, NUM_BUCKETS), const),
           pl.BlockSpec((1, VD), const)],
        out_specs=pl.BlockSpec((1, TPAD, H_A * VD), per_b),
        scratch_shapes=[pltpu.VMEM((SROWS, 1), F32), pltpu.VMEM((SROWS, 1), F32), pltpu.VMEM((SROWS, VD), F32)],
    )
    return pl.pallas_call(
        functools.partial(_sample_attn_kernel, pp=pp, n_pages=n_pages, ts=ts, out_scale=out_scale),
        grid_spec=grid_spec,
        out_shape=jax.ShapeDtypeStruct((db, TPAD, H_A * VD), BF16),
        compiler_params=_cparams(("parallel", "arbitrary")),
        name="sample_attn",
    )(page_table.reshape(-1), lam, wq, *([cache_kt] * pp), *([cache_v2] * pp), knew_t, vnew, far_col, tbl, subln)


def _hgrn_consts(chunk):
    t = np.arange(chunk)
    rows = []
    for m in HGRN_LEVELS[1:]:
        same = (t[:, None] // m) == (t[None, :] // m)
        rows.append(same & (t[None, :] <= t[:, None]))
        rows.append(same & (t[None, :] > t[:, None]))
    cum = np.concatenate(rows, axis=0).astype(np.float32)
    cum = np.concatenate([cum, cum, cum], axis=1)
    masks = []
    for m in HGRN_LEVELS[:-1]:
        bt, bs = t[:, None] // m, t[None, :] // m
        masks.append((bt == bs + 1) & (bs % 2 == 0))
    masks.append(t[:, None] == t[None, :])
    return jnp.asarray(cum, BF16), jnp.asarray(np.stack(masks).astype(np.float32))


def _hgrn_kernel(q_ref, f_ref, i_ref, g_ref, lb_ref, gain_ref, s0_ref, cum_ref, mask_ref,
                 o_ref, s_out_ref, st_ref, *, tblk, valid_len):
    c = HGRN_CHUNK
    nlev = len(HGRN_LEVELS)
    tstep = pl.program_id(1)

    @pl.when(tstep == 0)
    def _():
        for h in range(H_B):
            st_ref[h] = jnp.transpose(s0_ref[0, h])

    lb = lb_ref[...]
    cum = cum_ref[...]
    for ci in range(tblk // c):
        rows = slice(ci * c, (ci + 1) * c)
        f = lb + (1.0 - lb) * jax.nn.sigmoid(f_ref[0, rows, :])
        g = jnp.log(jnp.maximum(f, F_MIN))
        kh = 1.0 - f
        if valid_len is not None:
            pos = tstep * tblk + ci * c + lax.broadcasted_iota(jnp.int32, g.shape, 0)
            g = jnp.where(pos < valid_len, g, 0.0)
            kh = jnp.where(pos < valid_len, kh, 0.0)
        g_hi = g.astype(BF16)
        r1 = g - g_hi.astype(F32)
        g_mid = r1.astype(BF16)
        g_lo = (r1 - g_mid.astype(F32)).astype(BF16)
        ce = jnp.dot(cum, jnp.concatenate([g_hi, g_mid, g_lo], axis=0), preferred_element_type=F32)
        for h in range(H_B):
            lanes = slice(h * DK, (h + 1) * DK)
            q = q_ref[0, rows, lanes]
            k = kh[:, lanes]
            vb = i_ref[0, rows, lanes].astype(BF16)
            kb = k.astype(BF16)
            a = jnp.where(mask_ref[nlev - 1] > 0.5, _dot_nt(q.astype(BF16), kb), 0.0)
            a = a + jnp.where(mask_ref[0] > 0.5, _dot_nt((q * jnp.exp(g[:, lanes])).astype(BF16), kb), 0.0)
            for lv in range(1, nlev - 1):
                cm = ce[(2 * lv - 2) * c:(2 * lv - 1) * c, lanes]
                em = ce[(2 * lv - 1) * c:(2 * lv) * c, lanes]
                qt = (q * jnp.exp(cm)).astype(BF16)
                kt = (k * jnp.exp(em)).astype(BF16)
                a = a + jnp.where(mask_ref[lv] > 0.5, _dot_nt(qt, kt), 0.0)
            b_incl = ce[(2 * nlev - 4) * c:(2 * nlev - 3) * c, lanes]
            b_rest = ce[(2 * nlev - 3) * c:(2 * nlev - 2) * c, lanes]
            st = st_ref[h]
            o = _dot_nt((q * jnp.exp(b_incl)).astype(BF16), st.astype(BF16))
            o = o + jnp.dot(a.astype(BF16), vb, preferred_element_type=F32)
            k_end = (k * jnp.exp(b_rest)).astype(BF16)
            vt = jnp.transpose(i_ref[0, rows, lanes]).astype(BF16)
            decay_end = jnp.exp(b_incl[c - 1:c, :])
            st_ref[h] = st * decay_end + jnp.dot(vt, k_end, preferred_element_type=F32)
            gate = g_ref[0, rows, lanes]
            o = _rms(o, gain_ref[...]) * (gate * jax.nn.sigmoid(gate))
            o_ref[0, rows, lanes] = o.astype(BF16)

    @pl.when(tstep == pl.num_programs(1) - 1)
    def _():
        for h in range(H_B):
            s_out_ref[0, h] = jnp.transpose(st_ref[h])


def _hgrn(qh, fh, ih, gh, lb, gain, s0, tblk, valid_len):
    b, t, w = qh.shape
    cum, masks = _hgrn_consts(HGRN_CHUNK)
    blk = lambda bb, i: (bb, i, 0)
    const2 = lambda bb, i: (0, 0)
    spec = pl.BlockSpec((1, tblk, w), blk)
    s_spec = pl.BlockSpec((1, H_B, DK, DV), lambda bb, i: (bb, 0, 0, 0))
    return pl.pallas_call(
        functools.partial(_hgrn_kernel, tblk=tblk, valid_len=valid_len),
        grid=(b, t // tblk),
        in_specs=[spec, spec, spec, spec, pl.BlockSpec((1, w), const2), pl.BlockSpec((1, DV), const2), s_spec,
                  pl.BlockSpec(cum.shape, const2), pl.BlockSpec(masks.shape, lambda bb, i: (0, 0, 0))],
        out_specs=[spec, s_spec],
        out_shape=[jax.ShapeDtypeStruct((b, t, w), BF16), jax.ShapeDtypeStruct((b, H_B, DK, DV), F32)],
        scratch_shapes=[pltpu.VMEM((H_B, DV, DK), F32)],
        compiler_params=_cparams(("parallel", "arbitrary")),
        name="hgrn",
    )(qh, fh, ih, gh, lb, gain, s0, cum, masks)


def _memkv_kernel(x_ref, ln_ref, wk_ref, wv_ref, gk_ref, k_ref, v_ref, kb_ref, vb_ref):
    mb = _rms(x_ref[...], ln_ref[...]).astype(BF16)
    k = jnp.dot(mb, wk_ref[...], preferred_element_type=F32)
    for h in range(MH):
        lanes = slice(h * MDH, (h + 1) * MDH)
        kn = _rms(k[:, lanes], gk_ref[...])
        k_ref[:, lanes] = kn
        kb_ref[:, lanes] = kn.astype(BF16)
    v = jnp.dot(mb, wv_ref[...], preferred_element_type=F32)
    v_ref[...] = v
    vb_ref[...] = v.astype(BF16)


def _memkv(mem2d, ln, wk_b, wv_b, gk, tm):
    n, d = mem2d.shape
    w = MH * MDH
    row = lambda i: (i, 0)
    const = lambda i: (0, 0)
    o_spec = pl.BlockSpec((tm, w), row)
    return pl.pallas_call(
        _memkv_kernel,
        grid=(n // tm,),
        in_specs=[pl.BlockSpec((tm, d), row), pl.BlockSpec((1, d), const), pl.BlockSpec((d, w), const),
                  pl.BlockSpec((d, w), const), pl.BlockSpec((1, MDH), const)],
        out_specs=[o_spec] * 4,
        out_shape=[jax.ShapeDtypeStruct((n, w), F32), jax.ShapeDtypeStruct((n, w), F32),
                   jax.ShapeDtypeStruct((n, w), BF16), jax.ShapeDtypeStruct((n, w), BF16)],
        compiler_params=_cparams(("parallel",)),
        name="memkv",
    )(mem2d, ln, wk_b, wv_b, gk)


def _mid_kernel(x_ref, oa_ref, ob_ref, woa_ref, wob_ref, ln_ref, wq_ref, gq_ref, mk_ref, mv_ref, wo_ref,
                y_ref, qm_ref, om_ref, *, bb, tt):
    x = x_ref[...]
    x = x + jnp.dot(oa_ref[...], woa_ref[...], preferred_element_type=F32) + jnp.dot(
        ob_ref[...], wob_ref[...], preferred_element_type=F32)
    hb = _rms(x, ln_ref[...]).astype(BF16)
    qm = jnp.dot(hb, wq_ref[...], preferred_element_type=F32)
    for h in range(MH):
        lanes = slice(h * MDH, (h + 1) * MDH)
        qm_ref[:, lanes] = _rms(qm[:, lanes], gq_ref[...]) * (MDH ** -0.5)

    def per_batch(bi, carry):
        r = pl.ds(pl.multiple_of(bi * tt, tt), tt)
        for h in range(MH):
            lanes = slice(h * MDH, (h + 1) * MDH)
            s = _dot_nt(qm_ref[r, lanes].astype(BF16), mk_ref[bi, :, lanes])
            s = s - jnp.max(s, axis=-1, keepdims=True)
            p = jnp.exp(s)
            p = p / jnp.sum(p, axis=-1, keepdims=True)
            om_ref[r, lanes] = jnp.dot(p.astype(BF16), mv_ref[bi, :, lanes], preferred_element_type=F32)
        return carry

    lax.fori_loop(0, bb, per_batch, 0)
    y_ref[...] = x + jnp.dot(om_ref[...].astype(BF16), wo_ref[...], preferred_element_type=F32)


def _mid(x2d, oa, ob, woa, wob, ln, wq, gq, mkb, mvb, wo, t, bb, tt):
    n, d = x2d.shape
    n_mem = mkb.shape[1]
    w = H_A * VD
    assert bb == 1 or tt == t
    per_batch = t // tt
    row = lambda i: (i, 0)
    const = lambda i: (0, 0)
    mem_spec = pl.BlockSpec((bb, n_mem, MH * MDH), lambda i: (i // per_batch, 0, 0))
    rows = bb * tt
    return pl.pallas_call(
        functools.partial(_mid_kernel, bb=bb, tt=tt),
        grid=(n // rows,),
        in_specs=[pl.BlockSpec((rows, d), row), pl.BlockSpec((rows, w), row), pl.BlockSpec((rows, w), row),
                  pl.BlockSpec((w, d), const), pl.BlockSpec((w, d), const), pl.BlockSpec((1, d), const),
                  pl.BlockSpec((d, MH * MDH), const), pl.BlockSpec((1, MDH), const), mem_spec, mem_spec,
                  pl.BlockSpec((MH * MDH, d), const)],
        out_specs=pl.BlockSpec((rows, d), row),
        out_shape=jax.ShapeDtypeStruct((n, d), F32),
        scratch_shapes=[pltpu.VMEM((rows, MH * MDH), F32), pltpu.VMEM((rows, MH * MDH), F32)],
        compiler_params=_cparams(("parallel",)),
        name="mid",
    )(x2d, oa, ob, woa, wob, ln, wq, gq, mkb, mvb, wo)


LANE = 128


SUBLANES = 8


def _sort_network(n):
    pairs = []
    p = 1
    while p < n:
        k = p
        while k >= 1:
            for j in range(k % p, n - k, 2 * k):
                for i in range(min(k, n - j - k)):
                    if (i + j) // (2 * p) == (i + j + k) // (2 * p):
                        pairs.append((i + j, i + j + k))
            k //= 2
        p *= 2
    return pairs


def _sorted_tiles(tiles):
    wires = list(tiles) + [None] * (PEER_TOPK - len(tiles))
    for a, b in _sort_network(PEER_TOPK):
        if wires[a] is None:
            wires[a], wires[b] = wires[b], None
        elif wires[b] is not None:
            wires[a], wires[b] = jnp.maximum(wires[a], wires[b]), jnp.minimum(wires[a], wires[b])
    return [w for w in wires if w is not None]


def _largest_of_sorted(vs, count):
    out = []
    for r in range(count):
        m = jnp.max(vs[0], axis=0, keepdims=True)
        out.append(m)
        need = count - r - 1
        if need == 0:
            break
        eq = vs[0] == m
        vs = [jnp.where(eq, vs[i + 1] if i + 1 < len(vs) else -jnp.inf, vs[i]) for i in range(min(len(vs), need))]
    return out


def _split_tiles(x):
    return [x[i * SUBLANES:(i + 1) * SUBLANES] for i in range(x.shape[0] // SUBLANES)]


def _count_leading(rows_ref, pred):
    cnt = None
    for r in range(PEER_TOPK):
        hit = pred(rows_ref[r:r + 1, :])
        cnt = jnp.where(hit, float(r + 1), 0.0 if cnt is None else cnt)
    return cnt


def _peer_kernel(x_ref, ln_ref, wqt_ref, sk_ref, *refs, ec, n_chunks):
    u_refs = refs[:PEER_DMA_SPLIT]
    vtp_refs = refs[PEER_DMA_SPLIT:2 * PEER_DMA_SPLIT]
    (vtl_ref, y_ref, ht_ref, qt_ref, n_ref, e1_ref, r2_ref, w2_ref, acc_ref, s1_ref, s2_ref, v1_ref, v2_ref,
     wa_ref, wb_ref) = refs[2 * PEER_DMA_SPLIT:]
    cidx = pl.program_id(1)
    tt = x_ref.shape[0]

    @pl.when(cidx == 0)
    def _():
        hn = _rms(x_ref[...], ln_ref[...])
        ht_ref[...] = jnp.transpose(hn).astype(BF16)
        qt_ref[...] = jnp.dot(wqt_ref[...], ht_ref[...], preferred_element_type=F32).astype(BF16)
        acc_ref[...] = jnp.zeros(acc_ref.shape, F32)
        wb_ref[...] = jnp.zeros(wb_ref.shape, BF16)

        def per_head(hd, carry):
            def scores(p):
                r0 = pl.multiple_of((hd * 2 + p) * N_KEYS, N_KEYS)
                return jnp.dot(sk_ref[hd * 2 + p], qt_ref[pl.ds(r0, N_KEYS), :], preferred_element_type=F32)

            s1_ref[...] = scores(0)
            s2_ref[...] = scores(1)

            def per_lane_tile(lt, carry2):
                lanes = pl.ds(pl.multiple_of(lt * LANE, LANE), LANE)
                s1 = s1_ref[:, lanes]
                s2 = s2_ref[:, lanes]
                top1 = _largest_of_sorted(_sorted_tiles(_split_tiles(s1)), PEER_TOPK)
                top2 = _largest_of_sorted(_sorted_tiles(_split_tiles(s2)), PEER_TOPK)
                for r in range(PEER_TOPK):
                    v1_ref[r:r + 1, :] = top1[r]
                    v2_ref[r:r + 1, :] = top2[r]
                v2_lo = v2_ref[0:SUBLANES, :]
                cand = ([top1[0] + v2_lo, top1[0] + v2_ref[SUBLANES:PEER_TOPK, :]]
                        + [top1[i] + v2_lo for i in range(1, SUBLANES)]
                        + [v1_ref[SUBLANES:PEER_TOPK, :] + top2[0]])
                tau = _largest_of_sorted(_sorted_tiles(cand), PEER_TOPK)[PEER_TOPK - 1]
                top = top1[0] + top2[0]
                zsum = None
                for c in cand:
                    term = jnp.where(c >= tau, jnp.exp(c - top), 0.0)
                    zsum = term if zsum is None else zsum + term
                z = jnp.sum(zsum, axis=0, keepdims=True)
                th = tau - s1
                n_ref[hd, :, lanes] = _count_leading(v2_ref, lambda row: row >= th)
                e1_ref[hd, :, lanes] = jnp.exp(s1 - top1[0])
                r2_ref[hd, :, lanes] = _count_leading(v2_ref, lambda row: row > s2).astype(BF16)
                w2_ref[hd, :, lanes] = (jnp.exp(s2 - top2[0]) / z).astype(BF16)
                return carry2

            lax.fori_loop(0, tt // LANE, per_lane_tile, 0)
            return carry

        lax.fori_loop(0, PEER_HEADS, per_head, 0)

    n_a = ec // N_KEYS

    def step(w_cur_ref, w_prev_ref):
        vt_prev = jnp.concatenate([r[...] for r in vtp_refs], axis=0)
        acc_ref[...] += jnp.dot(vt_prev, w_prev_ref[...], preferred_element_type=F32)
        u_cur = jnp.concatenate([r[...] for r in u_refs], axis=0)
        act_all = jnp.dot(u_cur, ht_ref[...], preferred_element_type=F32)
        zero = jnp.zeros((N_KEYS, LANE), BF16)
        a0 = pl.multiple_of(cidx * n_a, n_a)
        n_rows = [n_ref[hd, pl.ds(a0, n_a), :] for hd in range(PEER_HEADS)]
        e1_rows = [e1_ref[hd, pl.ds(a0, n_a), :] for hd in range(PEER_HEADS)]
        for aa in range(n_a):
            for lt in range(tt // LANE):
                lanes = slice(lt * LANE, (lt + 1) * LANE)
                act = act_all[aa * N_KEYS:(aa + 1) * N_KEYS, lanes]
                act = 0.5 * act * (1.0 + lax.erf(act * SQRT_HALF))
                g = zero
                for hd in range(PEER_HEADS):
                    nb = jnp.broadcast_to(n_rows[hd][aa:aa + 1, lanes], (N_KEYS, LANE)).astype(BF16)
                    eb = jnp.broadcast_to(e1_rows[hd][aa:aa + 1, lanes], (N_KEYS, LANE)).astype(BF16)
                    g = g + jnp.where(r2_ref[hd, :, lanes] < nb, w2_ref[hd, :, lanes] * eb, zero)
                w_cur_ref[aa * N_KEYS:(aa + 1) * N_KEYS, lanes] = g * act.astype(BF16)

    @pl.when(cidx % 2 == 0)
    def _():
        step(wa_ref, wb_ref)

    @pl.when(cidx % 2 == 1)
    def _():
        step(wb_ref, wa_ref)

    @pl.when(cidx == n_chunks - 1)
    def _():
        w_last_ref = wa_ref if (n_chunks - 1) % 2 == 0 else wb_ref
        acc = acc_ref[...] + jnp.dot(vtl_ref[...], w_last_ref[...], preferred_element_type=F32)
        y_ref[...] = x_ref[...] + jnp.transpose(acc)


def _peer(x2d, ln, wqt_b, sk_b, u_b, vt_b, tt, ec):
    n, d = x2d.shape
    n_exp = u_b.shape[0]
    assert tt % LANE == 0 and ec % (8 * N_KEYS) == 0
    n_chunks = n_exp // ec
    row = lambda i, c: (i, 0)
    const = lambda i, c: (0, 0)
    head_f32 = pltpu.VMEM((PEER_HEADS, N_KEYS, tt), F32)
    head_b16 = pltpu.VMEM((PEER_HEADS, N_KEYS, tt), BF16)
    ns = PEER_DMA_SPLIT
    u_specs = [pl.BlockSpec((ec // ns, d), lambda i, c, k=k: (c * ns + k, 0)) for k in range(ns)]
    vtp_specs = [pl.BlockSpec((d // ns, ec), lambda i, c, k=k: (k, jnp.maximum(c - 1, 0))) for k in range(ns)]
    return pl.pallas_call(
        functools.partial(_peer_kernel, ec=ec, n_chunks=n_chunks),
        grid=(n // tt, n_chunks),
        in_specs=[pl.BlockSpec((tt, d), row), pl.BlockSpec((1, d), const), pl.BlockSpec(wqt_b.shape, const),
                  pl.BlockSpec(sk_b.shape, lambda i, c: (0, 0, 0))] + u_specs + vtp_specs
        + [pl.BlockSpec((d, ec), lambda i, c: (0, n_chunks - 1))],
        out_specs=pl.BlockSpec((tt, d), row),
        out_shape=jax.ShapeDtypeStruct((n, d), F32),
        scratch_shapes=[pltpu.VMEM((d, tt), BF16), pltpu.VMEM((wqt_b.shape[0], tt), BF16),
                        head_f32, head_f32, head_b16, head_b16,
                        pltpu.VMEM((d, tt), F32), pltpu.VMEM((N_KEYS, tt), F32), pltpu.VMEM((N_KEYS, tt), F32),
                        pltpu.VMEM((PEER_TOPK, LANE), F32), pltpu.VMEM((PEER_TOPK, LANE), F32),
                        pltpu.VMEM((ec, tt), BF16), pltpu.VMEM((ec, tt), BF16)],
        compiler_params=_cparams(("parallel", "arbitrary")),
        name="peer",
    )(x2d, ln, wqt_b, sk_b, *([u_b] * ns), *([vt_b] * ns), vt_b)


def _tile(n, pref):
    t = min(n, pref)
    while n % t:
        t //= 2
    return t


def _pad_axis(a, axis, size):
    pad = [(0, 0)] * a.ndim
    pad[axis] = (0, size - a.shape[axis])
    return jnp.pad(a, pad)


def kernel(x_prompt, x_sample, cache_k, cache_v, cache_mem_k, cache_mem_v, state_hgrn, page_table, mem_prompt,
           ln_mix, w_in, qk_gain, lambda_qk, subln_gain, rel_bias, lower_bounds, hgrn_norm, w_out, ln_mem,
           mem_norm, w_mq, w_mk, w_mv, mem_qk_gain, w_mo, ln_ffn, peer_wq, peer_subkeys, peer_u, peer_v):
    B, T, D = x_prompt.shape
    DB, TS, _ = x_sample.shape
    depth = ln_mix.shape[0]
    n_pages = page_table.shape[1]
    n_mem = mem_prompt.shape[1]
    n_phys = cache_k.shape[1]
    W = HA_COLS

    lbp = jax.nn.softmax(lower_bounds.astype(F32), axis=0)
    lbs = jnp.cumsum(lbp, axis=0) - lbp[0]
    p64 = jnp.asarray(np.kron(np.eye(W // DH), np.full((DH, DH), 1.0 / DH)), BF16)
    tq = _tile(T, 256)
    tiles = _bias_tiles(rel_bias, tq)
    cache_kt = jnp.transpose(cache_k, (0, 1, 3, 4, 5, 2)).reshape(depth, n_phys, W, PAGE_SIZE)
    cache_v2 = cache_v.reshape(depth, n_phys, PAGE_SIZE * H_A, VD)
    row_head = np.arange(SROWS) // (2 * TPAD)
    tbl = rel_bias[:, row_head].T
    far_col = tbl[:, NUM_BUCKETS - 1:]
    grp_mask = jnp.asarray((np.arange(W)[None, :] // DH == np.arange(W // DH)[:, None]).astype(np.float32), BF16)

    ts_pad = 8
    yp, ys = x_prompt, x_sample
    outs = {k: [] for k in ("kp", "vp", "mkp", "mvp", "sp", "ks", "vs", "ss")}
    for l in range(depth):
        lambda_init = 0.8 - 0.6 * math.exp(-0.3 * l)
        out_scale = 1.0 - lambda_init
        lq1, lk1, lq2, lk2 = lambda_qk[l].astype(F32)
        lam = (jnp.exp(jnp.sum(lq1 * lk1)) - jnp.exp(jnp.sum(lq2 * lk2)) + lambda_init).reshape(1)
        w_in_b = w_in[l].astype(BF16)
        gq = jnp.tile(qk_gain[l, 0], W // DH).reshape(1, W)
        gk = jnp.tile(qk_gain[l, 1], W // DH).reshape(1, W)
        ln_mix_l = ln_mix[l].reshape(1, D)
        subln = subln_gain[l].reshape(1, VD)
        lb = lbs[l].reshape(1, H_B * DK)
        hgain = hgrn_norm[l].reshape(1, DV)
        woa = w_out[l, :H_A * VD].astype(BF16)
        wob = w_out[l, H_A * VD:].astype(BF16)
        ln_mem_l = ln_mem[l].reshape(1, D)
        wmq = w_mq[l].astype(BF16)
        gmq = mem_qk_gain[l, 0].reshape(1, MDH)
        gmk = mem_qk_gain[l, 1].reshape(1, MDH)
        wmo = w_mo[l].astype(BF16)
        ln_ffn_l = ln_ffn[l].reshape(1, D)
        wqt = peer_wq[l].T.astype(BF16)
        sk = peer_subkeys[l].reshape(PEER_HEADS * 2, N_KEYS, -1).astype(BF16)
        u_b = peer_u[l].astype(BF16)
        vt_b = peer_v[l].T.astype(BF16)

        mk, mv, mkb, mvb = _memkv(mem_prompt.reshape(B * n_mem, D), mem_norm[l].reshape(1, D),
                                  w_mk[l].astype(BF16), w_mv[l].astype(BF16), gmk, _tile(B * n_mem, 512))
        qab, ka_t, va, kab, vab, qh, fh, ih, gh = _in_proj(yp.reshape(B * T, D), ln_mix_l, w_in_b, gq, gk, p64,
                                                           _tile(T, 512), seq_len=T)
        r3 = lambda a: a.reshape(B, T, W)
        oa = _prompt_attn(lam, rel_bias, r3(qab), r3(kab), r3(vab), tiles, subln, tq, out_scale)
        ob, sp = _hgrn(r3(qh), r3(fh), r3(ih), r3(gh), lb, hgain, jnp.zeros((B, H_B, DK, DV), F32),
                       _tile(T, 256), None)
        x2 = _mid(yp.reshape(B * T, D), oa.reshape(B * T, W), ob.reshape(B * T, W), woa, wob, ln_mem_l, wmq, gmq,
                  mkb.reshape(B, n_mem, W), mvb.reshape(B, n_mem, W), wmo, T, 1, _tile(T, 512))
        yp = _peer(x2, ln_ffn_l, wqt, sk, u_b, vt_b, _tile(B * T, 512), PEER_EC).reshape(B, T, D)
        outs["kp"].append(jnp.transpose(ka_t.reshape(B, H_A, 2, DH, T), (0, 4, 1, 2, 3)))
        outs["vp"].append(va.reshape(B, T, H_A, VD))
        outs["mkp"].append(mk.reshape(B, n_mem, MH, MDH))
        outs["mvp"].append(mv.reshape(B, n_mem, MH, MDH))
        outs["sp"].append(sp)

        qab, ka, va, kab, vab, qh, fh, ih, gh = _in_proj(ys.reshape(DB * TS, D), ln_mix_l, w_in_b, gq, gk, p64,
                                                         _tile(DB * TS, 512))
        s3 = lambda a: a.reshape(DB, TS, W)
        qs = _pad_axis(s3(qab), 1, TPAD)
        wq_s = (qs[:, None, :, :] * grp_mask[None, :, None, :]).reshape(DB, SROWS, W)
        knew_t = _pad_axis(jnp.swapaxes(s3(kab), 1, 2), 2, NEW_PAD)
        vnew = _pad_axis(s3(vab), 1, NEW_PAD)
        oa_s = _sample_attn(l, page_table, lam, wq_s, cache_kt, cache_v2, knew_t, vnew, far_col, tbl, subln,
                            _tile(n_pages, SAMPLE_PAGES_PER_STEP), TS, out_scale)
        hp = lambda a: _pad_axis(s3(a), 1, HGRN_CHUNK)
        ob_s, ss = _hgrn(hp(qh), hp(fh), hp(ih), hp(gh), lb, hgain, state_hgrn[l].astype(F32), HGRN_CHUNK, TS)
        x2 = _mid(_pad_axis(ys, 1, ts_pad).reshape(DB * ts_pad, D), oa_s.reshape(DB * ts_pad, W),
                  ob_s[:, :ts_pad].reshape(DB * ts_pad, W), woa, wob, ln_mem_l, wmq, gmq,
                  cache_mem_k[l].reshape(DB, n_mem, W).astype(BF16),
                  cache_mem_v[l].reshape(DB, n_mem, W).astype(BF16), wmo, ts_pad, DB, ts_pad)
        ys = _peer(x2, ln_ffn_l, wqt, sk, u_b, vt_b, DB * ts_pad, PEER_EC).reshape(DB, ts_pad, D)[:, :TS]
        outs["ks"].append(ka.reshape(DB, TS, H_A, 2, DH))
        outs["vs"].append(va.reshape(DB, TS, H_A, VD))
        outs["ss"].append(ss.astype(state_hgrn.dtype))

    st = lambda k: jnp.stack(outs[k])
    return (yp, ys, st("kp"), st("vp"), st("mkp"), st("mvp"), st("sp"), st("ks"), st("vs"), st("ss"))
```

```python
import functools
import math

import numpy as np
import jax
import jax.numpy as jnp
from jax import lax
from jax.experimental import pallas as pl
from jax.experimental.pallas import tpu as pltpu

F32 = jnp.float32
BF16 = jnp.bfloat16

PAGE_SIZE = 128
H_A = 4
DH = 64
VD = 2 * DH
H_B = 4
DK = 128
DV = 128
NUM_BUCKETS = 32
MAX_DISTANCE = 128
MH = 4
MDH = 128
PEER_HEADS = 8
N_KEYS = 128
PEER_TOPK = 16
EPS = 1e-6
NEG = -1e30
F_MIN = 1e-20
HA_COLS = H_A * 2 * DH
SQRT_HALF = 0.7071067811865476

VMEM_LIMIT_BYTES = 56 * 1024 * 1024
HGRN_CHUNK = 64
HGRN_LEVELS = (1, 2, 4, 8, 16, 32, 64)
PEER_EC = 1024
PEER_DMA_SPLIT = 4
SAMPLE_PAGES_PER_STEP = 16
PROMPT_HEADS_PER_STEP = 2


def _cparams(sem):
    return pltpu.CompilerParams(dimension_semantics=sem, vmem_limit_bytes=VMEM_LIMIT_BYTES)


def _rms(x, g):
    return x * lax.rsqrt(jnp.mean(x * x, axis=-1, keepdims=True) + EPS) * g


def _group_mean_sq(x, p_ref):
    sq = x * x
    hi = sq.astype(BF16)
    lo = (sq - hi.astype(F32)).astype(BF16)
    p = p_ref[...]
    return jnp.dot(hi, p, preferred_element_type=F32) + jnp.dot(lo, p, preferred_element_type=F32)


def _dot_nt(a, b):
    return lax.dot_general(a, b, (((1,), (1,)), ((), ())), preferred_element_type=F32)


def _t5_bucket(rel):
    n = jnp.maximum(rel, 0)
    max_exact = NUM_BUCKETS // 2
    nf = jnp.maximum(n, max_exact).astype(F32)
    large = max_exact + (jnp.log(nf / max_exact) / math.log(MAX_DISTANCE / max_exact)
                         * (NUM_BUCKETS - max_exact)).astype(jnp.int32)
    large = jnp.minimum(large, NUM_BUCKETS - 1)
    return jnp.where(n < max_exact, n, large)


def _in_proj_kernel(x_ref, ln_ref, w_ref, gq_ref, gk_ref, p_ref,
                    qab_ref, ka_ref, va_ref, kab_ref, vab_ref, qh_ref, fh_ref, ih_ref, gh_ref, *, k_transposed):
    hb = _rms(x_ref[...], ln_ref[...]).astype(BF16)

    def proj(j):
        return jnp.dot(hb, w_ref[:, j * HA_COLS:(j + 1) * HA_COLS], preferred_element_type=F32)

    qa = proj(0)
    qn = qa * lax.rsqrt(_group_mean_sq(qa, p_ref) + EPS) * gq_ref[...]
    qab_ref[...] = (qn * (DH ** -0.5)).astype(BF16)
    ka = proj(1)
    kn = ka * lax.rsqrt(_group_mean_sq(ka, p_ref) + EPS) * gk_ref[...]
    if k_transposed:
        ka_ref[0] = jnp.transpose(kn)
    else:
        ka_ref[...] = kn
    kab_ref[...] = kn.astype(BF16)
    va = proj(2)
    tm = va.shape[0]
    for h in range(H_A):
        va_ref[pl.ds(h, tm, stride=H_A), :] = va[:, h * VD:(h + 1) * VD]
    vab_ref[...] = va.astype(BF16)
    qh_ref[...] = proj(3)
    fh_ref[...] = proj(4)
    ih_ref[...] = proj(5)
    gh_ref[...] = proj(6)


def _in_proj(x2d, ln, w_b, gq, gk, p64, tm, seq_len=None):
    n, d = x2d.shape
    cols = w_b.shape[1]
    row = lambda i: (i, 0)
    const = lambda i: (0, 0)
    o_spec = pl.BlockSpec((tm, HA_COLS), row)
    f32o = jax.ShapeDtypeStruct((n, HA_COLS), F32)
    b16o = jax.ShapeDtypeStruct((n, HA_COLS), BF16)
    k_spec, k_shape = o_spec, f32o
    if seq_len is not None:
        per_seq = seq_len // tm
        k_spec = pl.BlockSpec((1, HA_COLS, tm), lambda i: (i // per_seq, 0, i % per_seq))
        k_shape = jax.ShapeDtypeStruct((n // seq_len, HA_COLS, seq_len), F32)
    return pl.pallas_call(
        functools.partial(_in_proj_kernel, k_transposed=seq_len is not None),
        grid=(n // tm,),
        in_specs=[pl.BlockSpec((tm, d), row), pl.BlockSpec((1, d), const), pl.BlockSpec((d, cols), const),
                  pl.BlockSpec((1, HA_COLS), const), pl.BlockSpec((1, HA_COLS), const),
                  pl.BlockSpec((HA_COLS, HA_COLS), const)],
        out_specs=[o_spec, k_spec, pl.BlockSpec((tm * H_A, VD), row)] + [o_spec] * 6,
        out_shape=[b16o, k_shape, jax.ShapeDtypeStruct((n * H_A, VD), F32), b16o, b16o, f32o, f32o, f32o, f32o],
        compiler_params=_cparams(("parallel",)),
        name="in_proj",
    )(x2d, ln, w_b, gq, gk, p64)


def _bias_tile_kernel(rb_ref, o_ref, *, tb):
    h = pl.program_id(0)
    d = pl.program_id(1)
    r = lax.broadcasted_iota(jnp.int32, (tb, tb), 0)
    c = lax.broadcasted_iota(jnp.int32, (tb, tb), 1)
    bucket = _t5_bucket(d * tb + r - c)
    acc = jnp.full((tb, tb), rb_ref[h], F32)
    for b in range(1, NUM_BUCKETS):
        acc = jnp.where(bucket == b, rb_ref[b * H_A + h], acc)
    o_ref[0, 0] = acc


def _bias_tiles(rel_bias, tb):
    return pl.pallas_call(
        functools.partial(_bias_tile_kernel, tb=tb),
        grid=(H_A, 2),
        in_specs=[pl.BlockSpec(memory_space=pltpu.SMEM)],
        out_specs=pl.BlockSpec((1, 1, tb, tb), lambda h, d: (h, d, 0, 0)),
        out_shape=jax.ShapeDtypeStruct((H_A, 2, tb, tb), F32),
        compiler_params=_cparams(("parallel", "parallel")),
        name="bias_tiles",
    )(rel_bias.reshape(-1))


def _prompt_attn_kernel(lam_ref, rb_ref, q_ref, k_ref, v_ref, bias_ref, g_ref, o_ref, *, tq, out_scale):
    hps = PROMPT_HEADS_PER_STEP
    hg = pl.program_id(1)
    i = pl.program_id(2)
    lane = lax.broadcasted_iota(jnp.int32, (tq, 2 * DH), 1)
    q2s, fars = [], []
    for hh in range(hps):
        q = q_ref[0, :, hh * 2 * DH:(hh + 1) * 2 * DH]
        zero = jnp.zeros_like(q)
        q2s.append(jnp.concatenate([jnp.where(lane < DH, q, zero), jnp.where(lane >= DH, q, zero)], axis=0))
        fars.append(rb_ref[(NUM_BUCKETS - 1) * H_A + hg * hps + hh])

    def update(carry, j, s_fns):
        rows = pl.ds(pl.multiple_of(j * tq, tq), tq)
        out = []
        for hh in range(hps):
            m, l, acc = carry[hh]
            kj = k_ref[0, rows, hh * 2 * DH:(hh + 1) * 2 * DH]
            vj = v_ref[0, rows, hh * VD:(hh + 1) * VD]
            s = s_fns[hh](_dot_nt(q2s[hh], kj))
            m_new = jnp.maximum(m, jnp.max(s, axis=-1, keepdims=True))
            alpha = jnp.exp(m - m_new)
            p = jnp.exp(s - m_new)
            l = alpha * l + jnp.sum(p, axis=-1, keepdims=True)
            acc = alpha * acc + jnp.dot(p.astype(BF16), vj, preferred_element_type=F32)
            out.append((m_new, l, acc))
        return tuple(out)

    init = tuple((jnp.full((2 * tq, 1), NEG, F32), jnp.zeros((2 * tq, 1), F32), jnp.zeros((2 * tq, VD), F32))
                 for _ in range(hps))
    far_fns = [lambda s, fb=fb: s + fb for fb in fars]
    carry = lax.fori_loop(0, jnp.maximum(i - 1, 0), lambda j, c: update(c, j, far_fns), init)
    near_fns = []
    for hh in range(hps):
        b1 = bias_ref[hh, 1]
        b1 = jnp.concatenate([b1, b1], axis=0)
        near_fns.append(lambda s, b1=b1: jnp.where(i >= 1, s + b1, NEG))
    carry = update(carry, jnp.maximum(i - 1, 0), near_fns)
    r = lax.broadcasted_iota(jnp.int32, (tq, tq), 0)
    c = lax.broadcasted_iota(jnp.int32, (tq, tq), 1)
    diag_fns = []
    for hh in range(hps):
        b0 = jnp.where(r >= c, bias_ref[hh, 0], NEG)
        b0 = jnp.concatenate([b0, b0], axis=0)
        diag_fns.append(lambda s, b0=b0: jnp.where(b0 > 0.5 * NEG, s + b0, NEG))
    carry = update(carry, i, diag_fns)
    for hh in range(hps):
        m, l, acc = carry[hh]
        o = acc / l
        o = o[:tq] - lam_ref[0] * o[tq:]
        o_ref[0, :, hh * VD:(hh + 1) * VD] = (_rms(o, g_ref[...]) * out_scale).astype(BF16)


def _prompt_attn(lam, rel_bias, qab, kab, vab, tiles, subln, tq, out_scale):
    b, t, _ = qab.shape
    hps = PROMPT_HEADS_PER_STEP
    blk = lambda bb, h, i: (bb, i, h)
    full = lambda bb, h, i: (bb, 0, h)
    return pl.pallas_call(
        functools.partial(_prompt_attn_kernel, tq=tq, out_scale=out_scale),
        grid=(b, H_A // hps, t // tq),
        in_specs=[pl.BlockSpec(memory_space=pltpu.SMEM), pl.BlockSpec(memory_space=pltpu.SMEM),
                  pl.BlockSpec((1, tq, hps * 2 * DH), blk), pl.BlockSpec((1, t, hps * 2 * DH), full),
                  pl.BlockSpec((1, t, hps * VD), full),
                  pl.BlockSpec((hps, 2, tq, tq), lambda bb, h, i: (h, 0, 0, 0)),
                  pl.BlockSpec((1, VD), lambda bb, h, i: (0, 0))],
        out_specs=pl.BlockSpec((1, tq, hps * VD), blk),
        out_shape=jax.ShapeDtypeStruct((b, t, H_A * VD), BF16),
        compiler_params=_cparams(("parallel", "parallel", "arbitrary")),
        name="prompt_attn",
    )(lam, rel_bias.reshape(-1), qab, kab, vab, tiles, subln)


TPAD = 8
SROWS = H_A * 2 * TPAD
NEW_PAD = 128


def _sample_attn_kernel(pt_ref, lam_ref, wq_ref, *refs, pp, n_pages, ts, out_scale):
    k_refs = refs[:pp]
    v_refs = refs[pp:2 * pp]
    (knew_ref, vnew_ref, far_ref, tbl_ref, g_ref, o_ref, m_ref, l_ref, acc_ref) = refs[2 * pp:]
    j = pl.program_id(1)
    nsteps = n_pages // pp
    past = n_pages * PAGE_SIZE

    @pl.when(j == 0)
    def _():
        m_ref[...] = jnp.full(m_ref.shape, NEG, F32)
        l_ref[...] = jnp.zeros(l_ref.shape, F32)
        acc_ref[...] = jnp.zeros(acc_ref.shape, F32)

    wq = wq_ref[0]

    def update(s, v_heads):
        m = m_ref[...]
        m_new = jnp.maximum(m, jnp.max(s, axis=1, keepdims=True))
        alpha = jnp.exp(m - m_new)
        p = jnp.exp(s - m_new)
        l_ref[...] = alpha * l_ref[...] + jnp.sum(p, axis=1, keepdims=True)
        m_ref[...] = m_new
        pb = p.astype(BF16)
        for h in range(H_A):
            rows = slice(h * 2 * TPAD, (h + 1) * 2 * TPAD)
            acc_ref[rows, :] = acc_ref[rows, :] * alpha[rows] + jnp.dot(pb[rows], v_heads[h],
                                                                        preferred_element_type=F32)

    def near_bias(rel):
        bucket = _t5_bucket(rel)
        acc = jnp.broadcast_to(tbl_ref[:, 0:1], rel.shape)
        for b in range(1, NUM_BUCKETS):
            acc = jnp.where(bucket == b, tbl_ref[:, b:b + 1], acc)
        return acc

    def tok_of_row(shape):
        return lax.broadcasted_iota(jnp.int32, shape, 0) % TPAD

    far = far_ref[...]
    s_pages = [jnp.dot(wq, k_refs[u][0, 0].astype(BF16), preferred_element_type=F32) for u in range(pp)]
    v_heads = [jnp.concatenate([v_refs[u][0, 0, pl.ds(h, PAGE_SIZE, stride=H_A), :].astype(BF16)
                                for u in range(pp)], axis=0) for h in range(H_A)]

    @pl.when(j < nsteps - 1)
    def _():
        update(jnp.concatenate(s_pages, axis=1) + far, v_heads)

    @pl.when(j == nsteps - 1)
    def _():
        shape = (SROWS, PAGE_SIZE)
        kpos = (n_pages - 1) * PAGE_SIZE + lax.broadcasted_iota(jnp.int32, shape, 1)
        rel = past + tok_of_row(shape) - kpos
        s_last = s_pages[pp - 1] + near_bias(rel)
        update(jnp.concatenate([sp + far for sp in s_pages[:pp - 1]] + [s_last], axis=1), v_heads)
        shape = (SROWS, NEW_PAD)
        kt = lax.broadcasted_iota(jnp.int32, shape, 1)
        rel = tok_of_row(shape) - kt
        s_new = jnp.dot(wq, knew_ref[0], preferred_element_type=F32) + near_bias(rel)
        s_new = jnp.where((rel >= 0) & (kt < ts), s_new, NEG)
        update(s_new, [vnew_ref[0, :, h * VD:(h + 1) * VD] for h in range(H_A)])
        o = acc_ref[...] / l_ref[...]
        lam = lam_ref[0]
        for h in range(H_A):
            r0 = h * 2 * TPAD
            oh = o[r0:r0 + TPAD] - lam * o[r0 + TPAD:r0 + 2 * TPAD]
            o_ref[0, :, h * VD:(h + 1) * VD] = (_rms(oh, g_ref[...]) * out_scale).astype(o_ref.dtype)


def _sample_attn(layer, page_table, lam, wq, cache_kt, cache_v2, knew_t, vnew, far_col, tbl, subln, pp, ts,
                 out_scale):
    db, n_pages = page_table.shape
    nsteps = n_pages // pp

    def page_map(u):
        return lambda b, j, pt: (layer, pt[b * n_pages + j * pp + u], 0, 0)

    per_b = lambda b, j, pt: (b, 0, 0)
    const = lambda b, j, pt: (0, 0)
    page_spec = [pl.BlockSpec((1, 1, HA_COLS, PAGE_SIZE), page_map(u)) for u in range(pp)]
    grid_spec = pltpu.PrefetchScalarGridSpec(
        num_scalar_prefetch=1,
        grid=(db, nsteps),
        in_specs=[pl.BlockSpec(memory_space=pltpu.SMEM), pl.BlockSpec((1, SROWS, HA_COLS), per_b)]
        + page_spec + page_spec
        + [pl.BlockSpec((1, HA_COLS, NEW_PAD), per_b), pl.BlockSpec((1, NEW_PAD, H_A * VD), per_b),
           pl.BlockSpec((SROWS, 1), const), pl.BlockSpec((SROWS, NUM_BUCKETS), const),
           pl.BlockSpec((1, VD), const)],
        out_specs=pl.BlockSpec((1, TPAD, H_A * VD), per_b),
        scratch_shapes=[pltpu.VMEM((SROWS, 1), F32), pltpu.VMEM((SROWS, 1), F32), pltpu.VMEM((SROWS, VD), F32)],
    )
    return pl.pallas_call(
        functools.partial(_sample_attn_kernel, pp=pp, n_pages=n_pages, ts=ts, out_scale=out_scale),
        grid_spec=grid_spec,
        out_shape=jax.ShapeDtypeStruct((db, TPAD, H_A * VD), BF16),
        compiler_params=_cparams(("parallel", "arbitrary")),
        name="sample_attn",
    )(page_table.reshape(-1), lam, wq, *([cache_kt] * pp), *([cache_v2] * pp), knew_t, vnew, far_col, tbl, subln)


def _hgrn_consts(chunk):
    t = np.arange(chunk)
    rows = []
    for m in HGRN_LEVELS[1:]:
        same = (t[:, None] // m) == (t[None, :] // m)
        rows.append(same & (t[None, :] <= t[:, None]))
        rows.append(same & (t[None, :] > t[:, None]))
    cum = np.concatenate(rows, axis=0).astype(np.float32)
    cum = np.concatenate([cum, cum, cum], axis=1)
    masks = []
    for m in HGRN_LEVELS[:-1]:
        bt, bs = t[:, None] // m, t[None, :] // m
        masks.append((bt == bs + 1) & (bs % 2 == 0))
    masks.append(t[:, None] == t[None, :])
    return jnp.asarray(cum, BF16), jnp.asarray(np.stack(masks).astype(np.float32))


def _hgrn_kernel(q_ref, f_ref, i_ref, g_ref, lb_ref, gain_ref, s0_ref, cum_ref, mask_ref,
                 o_ref, s_out_ref, st_ref, *, tblk, valid_len):
    c = HGRN_CHUNK
    nlev = len(HGRN_LEVELS)
    tstep = pl.program_id(1)

    @pl.when(tstep == 0)
    def _():
        for h in range(H_B):
            st_ref[h] = jnp.transpose(s0_ref[0, h])

    lb = lb_ref[...]
    cum = cum_ref[...]
    for ci in range(tblk // c):
        rows = slice(ci * c, (ci + 1) * c)
        f = lb + (1.0 - lb) * jax.nn.sigmoid(f_ref[0, rows, :])
        g = jnp.log(jnp.maximum(f, F_MIN))
        kh = 1.0 - f
        if valid_len is not None:
            pos = tstep * tblk + ci * c + lax.broadcasted_iota(jnp.int32, g.shape, 0)
            g = jnp.where(pos < valid_len, g, 0.0)
            kh = jnp.where(pos < valid_len, kh, 0.0)
        g_hi = g.astype(BF16)
        r1 = g - g_hi.astype(F32)
        g_mid = r1.astype(BF16)
        g_lo = (r1 - g_mid.astype(F32)).astype(BF16)
        ce = jnp.dot(cum, jnp.concatenate([g_hi, g_mid, g_lo], axis=0), preferred_element_type=F32)
        for h in range(H_B):
            lanes = slice(h * DK, (h + 1) * DK)
            q = q_ref[0, rows, lanes]
            k = kh[:, lanes]
            vb = i_ref[0, rows, lanes].astype(BF16)
            kb = k.astype(BF16)
            a = jnp.where(mask_ref[nlev - 1] > 0.5, _dot_nt(q.astype(BF16), kb), 0.0)
            a = a + jnp.where(mask_ref[0] > 0.5, _dot_nt((q * jnp.exp(g[:, lanes])).astype(BF16), kb), 0.0)
            for lv in range(1, nlev - 1):
                cm = ce[(2 * lv - 2) * c:(2 * lv - 1) * c, lanes]
                em = ce[(2 * lv - 1) * c:(2 * lv) * c, lanes]
                qt = (q * jnp.exp(cm)).astype(BF16)
                kt = (k * jnp.exp(em)).astype(BF16)
                a = a + jnp.where(mask_ref[lv] > 0.5, _dot_nt(qt, kt), 0.0)
            b_incl = ce[(2 * nlev - 4) * c:(2 * nlev - 3) * c, lanes]
            b_rest = ce[(2 * nlev - 3) * c:(2 * nlev - 2) * c, lanes]
            st = st_ref[h]
            o = _dot_nt((q * jnp.exp(b_incl)).astype(BF16), st.astype(BF16))
            o = o + jnp.dot(a.astype(BF16), vb, preferred_element_type=F32)
            k_end = (k * jnp.exp(b_rest)).astype(BF16)
            vt = jnp.transpose(i_ref[0, rows, lanes]).astype(BF16)
            decay_end = jnp.exp(b_incl[c - 1:c, :])
            st_ref[h] = st * decay_end + jnp.dot(vt, k_end, preferred_element_type=F32)
            gate = g_ref[0, rows, lanes]
            o = _rms(o, gain_ref[...]) * (gate * jax.nn.sigmoid(gate))
            o_ref[0, rows, lanes] = o.astype(BF16)

    @pl.when(tstep == pl.num_programs(1) - 1)
    def _():
        for h in range(H_B):
            s_out_ref[0, h] = jnp.transpose(st_ref[h])


def _hgrn(qh, fh, ih, gh, lb, gain, s0, tblk, valid_len):
    b, t, w = qh.shape
    cum, masks = _hgrn_consts(HGRN_CHUNK)
    blk = lambda bb, i: (bb, i, 0)
    const2 = lambda bb, i: (0, 0)
    spec = pl.BlockSpec((1, tblk, w), blk)
    s_spec = pl.BlockSpec((1, H_B, DK, DV), lambda bb, i: (bb, 0, 0, 0))
    return pl.pallas_call(
        functools.partial(_hgrn_kernel, tblk=tblk, valid_len=valid_len),
        grid=(b, t // tblk),
        in_specs=[spec, spec, spec, spec, pl.BlockSpec((1, w), const2), pl.BlockSpec((1, DV), const2), s_spec,
                  pl.BlockSpec(cum.shape, const2), pl.BlockSpec(masks.shape, lambda bb, i: (0, 0, 0))],
        out_specs=[spec, s_spec],
        out_shape=[jax.ShapeDtypeStruct((b, t, w), BF16), jax.ShapeDtypeStruct((b, H_B, DK, DV), F32)],
        scratch_shapes=[pltpu.VMEM((H_B, DV, DK), F32)],
        compiler_params=_cparams(("parallel", "arbitrary")),
        name="hgrn",
    )(qh, fh, ih, gh, lb, gain, s0, cum, masks)


def _memkv_kernel(x_ref, ln_ref, wk_ref, wv_ref, gk_ref, k_ref, v_ref, kb_ref, vb_ref):
    mb = _rms(x_ref[...], ln_ref[...]).astype(BF16)
    k = jnp.dot(mb, wk_ref[...], preferred_element_type=F32)
    for h in range(MH):
        lanes = slice(h * MDH, (h + 1) * MDH)
        kn = _rms(k[:, lanes], gk_ref[...])
        k_ref[:, lanes] = kn
        kb_ref[:, lanes] = kn.astype(BF16)
    v = jnp.dot(mb, wv_ref[...], preferred_element_type=F32)
    v_ref[...] = v
    vb_ref[...] = v.astype(BF16)


def _memkv(mem2d, ln, wk_b, wv_b, gk, tm):
    n, d = mem2d.shape
    w = MH * MDH
    row = lambda i: (i, 0)
    const = lambda i: (0, 0)
    o_spec = pl.BlockSpec((tm, w), row)
    return pl.pallas_call(
        _memkv_kernel,
        grid=(n // tm,),
        in_specs=[pl.BlockSpec((tm, d), row), pl.BlockSpec((1, d), const), pl.BlockSpec((d, w), const),
                  pl.BlockSpec((d, w), const), pl.BlockSpec((1, MDH), const)],
        out_specs=[o_spec] * 4,
        out_shape=[jax.ShapeDtypeStruct((n, w), F32), jax.ShapeDtypeStruct((n, w), F32),
                   jax.ShapeDtypeStruct((n, w), BF16), jax.ShapeDtypeStruct((n, w), BF16)],
        compiler_params=_cparams(("parallel",)),
        name="memkv",
    )(mem2d, ln, wk_b, wv_b, gk)


def _mid_kernel(x_ref, oa_ref, ob_ref, woa_ref, wob_ref, ln_ref, wq_ref, gq_ref, mk_ref, mv_ref, wo_ref,
                y_ref, qm_ref, om_ref, *, bb, tt):
    x = x_ref[...]
    x = x + jnp.dot(oa_ref[...], woa_ref[...], preferred_element_type=F32) + jnp.dot(
        ob_ref[...], wob_ref[...], preferred_element_type=F32)
    hb = _rms(x, ln_ref[...]).astype(BF16)
    qm = jnp.dot(hb, wq_ref[...], preferred_element_type=F32)
    for h in range(MH):
        lanes = slice(h * MDH, (h + 1) * MDH)
        qm_ref[:, lanes] = _rms(qm[:, lanes], gq_ref[...]) * (MDH ** -0.5)

    def per_batch(bi, carry):
        r = pl.ds(pl.multiple_of(bi * tt, tt), tt)
        for h in range(MH):
            lanes = slice(h * MDH, (h + 1) * MDH)
            s = _dot_nt(qm_ref[r, lanes].astype(BF16), mk_ref[bi, :, lanes])
            s = s - jnp.max(s, axis=-1, keepdims=True)
            p = jnp.exp(s)
            p = p / jnp.sum(p, axis=-1, keepdims=True)
            om_ref[r, lanes] = jnp.dot(p.astype(BF16), mv_ref[bi, :, lanes], preferred_element_type=F32)
        return carry

    lax.fori_loop(0, bb, per_batch, 0)
    y_ref[...] = x + jnp.dot(om_ref[...].astype(BF16), wo_ref[...], preferred_element_type=F32)


def _mid(x2d, oa, ob, woa, wob, ln, wq, gq, mkb, mvb, wo, t, bb, tt):
    n, d = x2d.shape
    n_mem = mkb.shape[1]
    w = H_A * VD
    assert bb == 1 or tt == t
    per_batch = t // tt
    row = lambda i: (i, 0)
    const = lambda i: (0, 0)
    mem_spec = pl.BlockSpec((bb, n_mem, MH * MDH), lambda i: (i // per_batch, 0, 0))
    rows = bb * tt
    return pl.pallas_call(
        functools.partial(_mid_kernel, bb=bb, tt=tt),
        grid=(n // rows,),
        in_specs=[pl.BlockSpec((rows, d), row), pl.BlockSpec((rows, w), row), pl.BlockSpec((rows, w), row),
                  pl.BlockSpec((w, d), const), pl.BlockSpec((w, d), const), pl.BlockSpec((1, d), const),
                  pl.BlockSpec((d, MH * MDH), const), pl.BlockSpec((1, MDH), const), mem_spec, mem_spec,
                  pl.BlockSpec((MH * MDH, d), const)],
        out_specs=pl.BlockSpec((rows, d), row),
        out_shape=jax.ShapeDtypeStruct((n, d), F32),
        scratch_shapes=[pltpu.VMEM((rows, MH * MDH), F32), pltpu.VMEM((rows, MH * MDH), F32)],
        compiler_params=_cparams(("parallel",)),
        name="mid",
    )(x2d, oa, ob, woa, wob, ln, wq, gq, mkb, mvb, wo)


LANE = 128


SUBLANES = 8


def _sort_network(n):
    pairs = []
    p = 1
    while p < n:
        k = p
        while k >= 1:
            for j in range(k % p, n - k, 2 * k):
                for i in range(min(k, n - j - k)):
                    if (i + j) // (2 * p) == (i + j + k) // (2 * p):
                        pairs.append((i + j, i + j + k))
            k //= 2
        p *= 2
    return pairs


def _sorted_tiles(tiles):
    wires = list(tiles) + [None] * (PEER_TOPK - len(tiles))
    for a, b in _sort_network(PEER_TOPK):
        if wires[a] is None:
            wires[a], wires[b] = wires[b], None
        elif wires[b] is not None:
            wires[a], wires[b] = jnp.maximum(wires[a], wires[b]), jnp.minimum(wires[a], wires[b])
    return [w for w in wires if w is not None]


def _largest_of_sorted(vs, count):
    out = []
    for r in range(count):
        m = jnp.max(vs[0], axis=0, keepdims=True)
        out.append(m)
        need = count - r - 1
        if need == 0:
            break
        eq = vs[0] == m
        vs = [jnp.where(eq, vs[i + 1] if i + 1 < len(vs) else -jnp.inf, vs[i]) for i in range(min(len(vs), need))]
    return out


def _split_tiles(x):
    return [x[i * SUBLANES:(i + 1) * SUBLANES] for i in range(x.shape[0] // SUBLANES)]


def _count_leading(rows_ref, pred):
    cnt = None
    for r in range(PEER_TOPK):
        hit = pred(rows_ref[r:r + 1, :])
        cnt = jnp.where(hit, float(r + 1), 0.0 if cnt is None else cnt)
    return cnt


def _peer_kernel(x_ref, ln_ref, wqt_ref, sk_ref, *refs, ec, n_chunks):
    u_refs = refs[:PEER_DMA_SPLIT]
    vtp_refs = refs[PEER_DMA_SPLIT:2 * PEER_DMA_SPLIT]
    (vtl_ref, y_ref, ht_ref, qt_ref, n_ref, e1_ref, r2_ref, w2_ref, acc_ref, s1_ref, s2_ref, v1_ref, v2_ref,
     wa_ref, wb_ref) = refs[2 * PEER_DMA_SPLIT:]
    cidx = pl.program_id(1)
    tt = x_ref.shape[0]

    @pl.when(cidx == 0)
    def _():
        hn = _rms(x_ref[...], ln_ref[...])
        ht_ref[...] = jnp.transpose(hn).astype(BF16)
        qt_ref[...] = jnp.dot(wqt_ref[...], ht_ref[...], preferred_element_type=F32).astype(BF16)
        acc_ref[...] = jnp.zeros(acc_ref.shape, F32)
        wb_ref[...] = jnp.zeros(wb_ref.shape, BF16)

        def per_head(hd, carry):
            def scores(p):
                r0 = pl.multiple_of((hd * 2 + p) * N_KEYS, N_KEYS)
                return jnp.dot(sk_ref[hd * 2 + p], qt_ref[pl.ds(r0, N_KEYS), :], preferred_element_type=F32)

            s1_ref[...] = scores(0)
            s2_ref[...] = scores(1)

            def per_lane_tile(lt, carry2):
                lanes = pl.ds(pl.multiple_of(lt * LANE, LANE), LANE)
                s1 = s1_ref[:, lanes]
                s2 = s2_ref[:, lanes]
                top1 = _largest_of_sorted(_sorted_tiles(_split_tiles(s1)), PEER_TOPK)
                top2 = _largest_of_sorted(_sorted_tiles(_split_tiles(s2)), PEER_TOPK)
                for r in range(PEER_TOPK):
                    v1_ref[r:r + 1, :] = top1[r]
                    v2_ref[r:r + 1, :] = top2[r]
                v2_lo = v2_ref[0:SUBLANES, :]
                cand = ([top1[0] + v2_lo, top1[0] + v2_ref[SUBLANES:PEER_TOPK, :]]
                        + [top1[i] + v2_lo for i in range(1, SUBLANES)]
                        + [v1_ref[SUBLANES:PEER_TOPK, :] + top2[0]])
                tau = _largest_of_sorted(_sorted_tiles(cand), PEER_TOPK)[PEER_TOPK - 1]
                top = top1[0] + top2[0]
                zsum = None
                for c in cand:
                    term = jnp.where(c >= tau, jnp.exp(c - top), 0.0)
                    zsum = term if zsum is None else zsum + term
                z = jnp.sum(zsum, axis=0, keepdims=True)
                th = tau - s1
                n_ref[hd, :, lanes] = _count_leading(v2_ref, lambda row: row >= th)
                e1_ref[hd, :, lanes] = jnp.exp(s1 - top1[0])
                r2_ref[hd, :, lanes] = _count_leading(v2_ref, lambda row: row > s2).astype(BF16)
                w2_ref[hd, :, lanes] = (jnp.exp(s2 - top2[0]) / z).astype(BF16)
                return carry2

            lax.fori_loop(0, tt // LANE, per_lane_tile, 0)
            return carry

        lax.fori_loop(0, PEER_HEADS, per_head, 0)

    n_a = ec // N_KEYS

    def step(w_cur_ref, w_prev_ref):
        vt_prev = jnp.concatenate([r[...] for r in vtp_refs], axis=0)
        acc_ref[...] += jnp.dot(vt_prev, w_prev_ref[...], preferred_element_type=F32)
        u_cur = jnp.concatenate([r[...] for r in u_refs], axis=0)
        act_all = jnp.dot(u_cur, ht_ref[...], preferred_element_type=F32)
        zero = jnp.zeros((N_KEYS, LANE), BF16)
        a0 = pl.multiple_of(cidx * n_a, n_a)
        n_rows = [n_ref[hd, pl.ds(a0, n_a), :] for hd in range(PEER_HEADS)]
        e1_rows = [e1_ref[hd, pl.ds(a0, n_a), :] for hd in range(PEER_HEADS)]
        for aa in range(n_a):
            for lt in range(tt // LANE):
                lanes = slice(lt * LANE, (lt + 1) * LANE)
                act = act_all[aa * N_KEYS:(aa + 1) * N_KEYS, lanes]
                act = 0.5 * act * (1.0 + lax.erf(act * SQRT_HALF))
                g = zero
                for hd in range(PEER_HEADS):
                    nb = jnp.broadcast_to(n_rows[hd][aa:aa + 1, lanes], (N_KEYS, LANE)).astype(BF16)
                    eb = jnp.broadcast_to(e1_rows[hd][aa:aa + 1, lanes], (N_KEYS, LANE)).astype(BF16)
                    g = g + jnp.where(r2_ref[hd, :, lanes] < nb, w2_ref[hd, :, lanes] * eb, zero)
                w_cur_ref[aa * N_KEYS:(aa + 1) * N_KEYS, lanes] = g * act.astype(BF16)

    @pl.when(cidx % 2 == 0)
    def _():
        step(wa_ref, wb_ref)

    @pl.when(cidx % 2 == 1)
    def _():
        step(wb_ref, wa_ref)

    @pl.when(cidx == n_chunks - 1)
    def _():
        w_last_ref = wa_ref if (n_chunks - 1) % 2 == 0 else wb_ref
        acc = acc_ref[...] + jnp.dot(vtl_ref[...], w_last_ref[...], preferred_element_type=F32)
        y_ref[...] = x_ref[...] + jnp.transpose(acc)


def _peer(x2d, ln, wqt_b, sk_b, u_b, vt_b, tt, ec):
    n, d = x2d.shape
    n_exp = u_b.shape[0]
    assert tt % LANE == 0 and ec % (8 * N_KEYS) == 0
    n_chunks = n_exp // ec
    row = lambda i, c: (i, 0)
    const = lambda i, c: (0, 0)
    head_f32 = pltpu.VMEM((PEER_HEADS, N_KEYS, tt), F32)
    head_b16 = pltpu.VMEM((PEER_HEADS, N_KEYS, tt), BF16)
    ns = PEER_DMA_SPLIT
    u_specs = [pl.BlockSpec((ec // ns, d), lambda i, c, k=k: (c * ns + k, 0)) for k in range(ns)]
    vtp_specs = [pl.BlockSpec((d // ns, ec), lambda i, c, k=k: (k, jnp.maximum(c - 1, 0))) for k in range(ns)]
    return pl.pallas_call(
        functools.partial(_peer_kernel, ec=ec, n_chunks=n_chunks),
        grid=(n // tt, n_chunks),
        in_specs=[pl.BlockSpec((tt, d), row), pl.BlockSpec((1, d), const), pl.BlockSpec(wqt_b.shape, const),
                  pl.BlockSpec(sk_b.shape, lambda i, c: (0, 0, 0))] + u_specs + vtp_specs
        + [pl.BlockSpec((d, ec), lambda i, c: (0, n_chunks - 1))],
        out_specs=pl.BlockSpec((tt, d), row),
        out_shape=jax.ShapeDtypeStruct((n, d), F32),
        scratch_shapes=[pltpu.VMEM((d, tt), BF16), pltpu.VMEM((wqt_b.shape[0], tt), BF16),
                        head_f32, head_f32, head_b16, head_b16,
                        pltpu.VMEM((d, tt), F32), pltpu.VMEM((N_KEYS, tt), F32), pltpu.VMEM((N_KEYS, tt), F32),
                        pltpu.VMEM((PEER_TOPK, LANE), F32), pltpu.VMEM((PEER_TOPK, LANE), F32),
                        pltpu.VMEM((ec, tt), BF16), pltpu.VMEM((ec, tt), BF16)],
        compiler_params=_cparams(("parallel", "arbitrary")),
        name="peer",
    )(x2d, ln, wqt_b, sk_b, *([u_b] * ns), *([vt_b] * ns), vt_b)


def _tile(n, pref):
    t = min(n, pref)
    while n % t:
        t //= 2
    return t


def _pad_axis(a, axis, size):
    pad = [(0, 0)] * a.ndim
    pad[axis] = (0, size - a.shape[axis])
    return jnp.pad(a, pad)


def kernel(x_prompt, x_sample, cache_k, cache_v, cache_mem_k, cache_mem_v, state_hgrn, page_table, mem_prompt,
           ln_mix, w_in, qk_gain, lambda_qk, subln_gain, rel_bias, lower_bounds, hgrn_norm, w_out, ln_mem,
           mem_norm, w_mq, w_mk, w_mv, mem_qk_gain, w_mo, ln_ffn, peer_wq, peer_subkeys, peer_u, peer_v):
    B, T, D = x_prompt.shape
    DB, TS, _ = x_sample.shape
    depth = ln_mix.shape[0]
    n_pages = page_table.shape[1]
    n_mem = mem_prompt.shape[1]
    n_phys = cache_k.shape[1]
    W = HA_COLS

    lbp = jax.nn.softmax(lower_bounds.astype(F32), axis=0)
    lbs = jnp.cumsum(lbp, axis=0) - lbp[0]
    p64 = jnp.asarray(np.kron(np.eye(W // DH), np.full((DH, DH), 1.0 / DH)), BF16)
    tq = _tile(T, 256)
    tiles = _bias_tiles(rel_bias, tq)
    cache_kt = jnp.transpose(cache_k, (0, 1, 3, 4, 5, 2)).reshape(depth, n_phys, W, PAGE_SIZE)
    cache_v2 = cache_v.reshape(depth, n_phys, PAGE_SIZE * H_A, VD)
    row_head = np.arange(SROWS) // (2 * TPAD)
    tbl = rel_bias[:, row_head].T
    far_col = tbl[:, NUM_BUCKETS - 1:]
    grp_mask = jnp.asarray((np.arange(W)[None, :] // DH == np.arange(W // DH)[:, None]).astype(np.float32), BF16)

    ts_pad = 8
    yp, ys = x_prompt, x_sample
    outs = {k: [] for k in ("kp", "vp", "mkp", "mvp", "sp", "ks", "vs", "ss")}
    for l in range(depth):
        lambda_init = 0.8 - 0.6 * math.exp(-0.3 * l)
        out_scale = 1.0 - lambda_init
        lq1, lk1, lq2, lk2 = lambda_qk[l].astype(F32)
        lam = (jnp.exp(jnp.sum(lq1 * lk1)) - jnp.exp(jnp.sum(lq2 * lk2)) + lambda_init).reshape(1)
        w_in_b = w_in[l].astype(BF16)
        gq = jnp.tile(qk_gain[l, 0], W // DH).reshape(1, W)
        gk = jnp.tile(qk_gain[l, 1], W // DH).reshape(1, W)
        ln_mix_l = ln_mix[l].reshape(1, D)
        subln = subln_gain[l].reshape(1, VD)
        lb = lbs[l].reshape(1, H_B * DK)
        hgain = hgrn_norm[l].reshape(1, DV)
        woa = w_out[l, :H_A * VD].astype(BF16)
        wob = w_out[l, H_A * VD:].astype(BF16)
        ln_mem_l = ln_mem[l].reshape(1, D)
        wmq = w_mq[l].astype(BF16)
        gmq = mem_qk_gain[l, 0].reshape(1, MDH)
        gmk = mem_qk_gain[l, 1].reshape(1, MDH)
        wmo = w_mo[l].astype(BF16)
        ln_ffn_l = ln_ffn[l].reshape(1, D)
        wqt = peer_wq[l].T.astype(BF16)
        sk = peer_subkeys[l].reshape(PEER_HEADS * 2, N_KEYS, -1).astype(BF16)
        u_b = peer_u[l].astype(BF16)
        vt_b = peer_v[l].T.astype(BF16)

        mk, mv, mkb, mvb = _memkv(mem_prompt.reshape(B * n_mem, D), mem_norm[l].reshape(1, D),
                                  w_mk[l].astype(BF16), w_mv[l].astype(BF16), gmk, _tile(B * n_mem, 512))
        qab, ka_t, va, kab, vab, qh, fh, ih, gh = _in_proj(yp.reshape(B * T, D), ln_mix_l, w_in_b, gq, gk, p64,
                                                           _tile(T, 512), seq_len=T)
        r3 = lambda a: a.reshape(B, T, W)
        oa = _prompt_attn(lam, rel_bias, r3(qab), r3(kab), r3(vab), tiles, subln, tq, out_scale)
        ob, sp = _hgrn(r3(qh), r3(fh), r3(ih), r3(gh), lb, hgain, jnp.zeros((B, H_B, DK, DV), F32),
                       _tile(T, 256), None)
        x2 = _mid(yp.reshape(B * T, D), oa.reshape(B * T, W), ob.reshape(B * T, W), woa, wob, ln_mem_l, wmq, gmq,
                  mkb.reshape(B, n_mem, W), mvb.reshape(B, n_mem, W), wmo, T, 1, _tile(T, 512))
        yp = _peer(x2, ln_ffn_l, wqt, sk, u_b, vt_b, _tile(B * T, 512), PEER_EC).reshape(B, T, D)
        outs["kp"].append(jnp.transpose(ka_t.reshape(B, H_A, 2, DH, T), (0, 4, 1, 2, 3)))
        outs["vp"].append(va.reshape(B, T, H_A, VD))
        outs["mkp"].append(mk.reshape(B, n_mem, MH, MDH))
        outs["mvp"].append(mv.reshape(B, n_mem, MH, MDH))
        outs["sp"].append(sp)

        qab, ka, va, kab, vab, qh, fh, ih, gh = _in_proj(ys.reshape(DB * TS, D), ln_mix_l, w_in_b, gq, gk, p64,
                                                         _tile(DB * TS, 512))
        s3 = lambda a: a.reshape(DB, TS, W)
        qs = _pad_axis(s3(qab), 1, TPAD)
        wq_s = (qs[:, None, :, :] * grp_mask[None, :, None, :]).reshape(DB, SROWS, W)
        knew_t = _pad_axis(jnp.swapaxes(s3(kab), 1, 2), 2, NEW_PAD)
        vnew = _pad_axis(s3(vab), 1, NEW_PAD)
        oa_s = _sample_attn(l, page_table, lam, wq_s, cache_kt, cache_v2, knew_t, vnew, far_col, tbl, subln,
                            _tile(n_pages, SAMPLE_PAGES_PER_STEP), TS, out_scale)
        hp = lambda a: _pad_axis(s3(a), 1, HGRN_CHUNK)
        ob_s, ss = _hgrn(hp(qh), hp(fh), hp(ih), hp(gh), lb, hgain, state_hgrn[l].astype(F32), HGRN_CHUNK, TS)
        x2 = _mid(_pad_axis(ys, 1, ts_pad).reshape(DB * ts_pad, D), oa_s.reshape(DB * ts_pad, W),
                  ob_s[:, :ts_pad].reshape(DB * ts_pad, W), woa, wob, ln_mem_l, wmq, gmq,
                  cache_mem_k[l].reshape(DB, n_mem, W).astype(BF16),
                  cache_mem_v[l].reshape(DB, n_mem, W).astype(BF16), wmo, ts_pad, DB, ts_pad)
        ys = _peer(x2, ln_ffn_l, wqt, sk, u_b, vt_b, DB * ts_pad, PEER_EC).reshape(DB, ts_pad, D)[:, :TS]
        outs["ks"].append(ka.reshape(DB, TS, H_A, 2, DH))
        outs["vs"].append(va.reshape(DB, TS, H_A, VD))
        outs["ss"].append(ss.astype(state_hgrn.dtype))

    st = lambda k: jnp.stack(outs[k])
    return (yp, ys, st("kp"), st("vp"), st("mkp"), st("mvp"), st("sp"), st("ks"), st("vs"), st("ss"))
```
